```python
import math
import jax, jax.numpy as jnp
from jax import lax
import numpy as np

D_MODEL = 1024
BATCH = 2
SEQ = 8192
DEPTH = 2

SSD_EXPAND = 2
D_INNER = SSD_EXPAND * D_MODEL
SSD_HEAD_DIM = 64
SSD_HEADS = D_INNER // SSD_HEAD_DIM
SSD_GROUPS = 8
SSD_STATE = 128
SSD_CONV = 5
SSD_CHUNK = 128
CONV_DIM = D_INNER + 2 * SSD_GROUPS * SSD_STATE
ATTN_HEAD_DIM = 64
ATTN_Q_HEADS = D_MODEL // ATTN_HEAD_DIM
ATTN_KV_HEADS = 4
ATTN_WIDTH = ATTN_Q_HEADS * ATTN_HEAD_DIM
KV_WIDTH = ATTN_KV_HEADS * ATTN_HEAD_DIM
WINDOW = 128
BLOCK = 128
N_BUCKETS = 32
MAX_DISTANCE = 128
D_FF = 4 * D_MODEL
EPS = 1e-6
IN_SPLITS = (D_INNER, CONV_DIM, 2 * SSD_HEADS, ATTN_WIDTH, KV_WIDTH, KV_WIDTH, 2 * D_MODEL)
N_IN = sum(IN_SPLITS)

kernel_name = 'hybrid_ssd_swa_encoder'


def rmsnorm(x, g):
    xf = x.astype(jnp.float32)
    y = xf * lax.rsqrt(jnp.mean(xf * xf, axis=-1, keepdims=True) + EPS)
    return (y * g.astype(jnp.float32)).astype(x.dtype)


def split_cols(t, sizes):
    idx, acc = [], 0
    for s in sizes[:-1]:
        acc += s
        idx.append(acc)
    return jnp.split(t, idx, axis=-1)


def depthwise_conv(u, w, b):
    pad = (SSD_CONV - 1) // 2
    y = lax.conv_general_dilated(u, w.astype(u.dtype), window_strides=(1,), padding=[(pad, pad)],
                                 dimension_numbers=('NWC', 'WIO', 'NWC'),
                                 feature_group_count=u.shape[-1])
    return y + b.astype(u.dtype)


def segsum_exp(a):
    cs = jnp.cumsum(a, axis=-1)
    diff = cs[..., :, None] - cs[..., None, :]
    L = a.shape[-1]
    mask = jnp.tril(jnp.ones((L, L), dtype=bool))
    return jnp.exp(jnp.where(mask, diff, -jnp.inf))


def ssd_scan(x, dt, a, b, c):
    Bsz, S, G, R, P = x.shape
    N = b.shape[-1]
    nc, L = S // SSD_CHUNK, SSD_CHUNK
    x = x.reshape(Bsz, nc, L, G, R, P)
    dt = dt.reshape(Bsz, nc, L, G, R)
    b = b.reshape(Bsz, nc, L, G, N)
    c = c.reshape(Bsz, nc, L, G, N)
    a_dt = jnp.moveaxis(dt * a, 2, -1)
    a_cs = jnp.cumsum(a_dt, axis=-1)
    xdt = x * dt[..., None]
    decay = segsum_exp(a_dt)
    cb = jnp.einsum('bclgn,bcsgn->bcgls', c, b)
    y_diag = jnp.einsum('bcgls,bcgrls,bcsgrp->bclgrp', cb, decay, xdt)
    decay_states = jnp.exp(a_cs[..., -1:] - a_cs)
    states = jnp.einsum('bclgn,bcgrl,bclgrp->bcgrpn', b, decay_states, xdt)
    chunk_decay = jnp.exp(a_cs[..., -1])

    def step(h, inp):
        st, dec = inp
        return h * dec[..., None, None] + st, h

    h0 = jnp.zeros((Bsz, G, R, P, N), x.dtype)
    _, prev = lax.scan(step, h0, (jnp.moveaxis(states, 1, 0), jnp.moveaxis(chunk_decay, 1, 0)))
    prev = jnp.moveaxis(prev, 0, 1)
    y_off = jnp.einsum('bclgn,bcgrpn,bcgrl->bclgrp', c, prev, jnp.exp(a_cs))
    return (y_diag + y_off).reshape(Bsz, S, G, R, P)


def gated_rmsnorm(y, z, w):
    u = y * jax.nn.silu(z.astype(jnp.float32))
    ug = u.reshape(u.shape[:-1] + (SSD_GROUPS, D_INNER // SSD_GROUPS))
    ug = ug * lax.rsqrt(jnp.mean(ug * ug, axis=-1, keepdims=True) + EPS)
    return ug.reshape(u.shape) * w.astype(jnp.float32)


def ssd_branch(z, xbc, dt_raw, conv_w, conv_b, dt_bias, a_log, d_skip, norm_w, w_out):
    Bsz, S = z.shape[:2]
    G, R, P, N = SSD_GROUPS, SSD_HEADS // SSD_GROUPS, SSD_HEAD_DIM, SSD_STATE
    xbc = jax.nn.silu(depthwise_conv(xbc, conv_w, conv_b))
    xs, bs, cs = split_cols(xbc, (D_INNER, G * N, G * N))
    xs = xs.astype(jnp.float32).reshape(Bsz, S, G, R, P)
    bs = bs.astype(jnp.float32).reshape(Bsz, S, G, N)
    cs = cs.astype(jnp.float32).reshape(Bsz, S, G, N)
    dt = jax.nn.softplus(dt_raw.astype(jnp.float32).reshape(Bsz, S, 2, G, R)
                         + dt_bias.astype(jnp.float32).reshape(2, G, R))
    a = -jnp.exp(a_log.astype(jnp.float32)).reshape(2, G, R)
    y_fwd = ssd_scan(xs, dt[:, :, 0], a[0], bs, cs)
    fl = lambda t: jnp.flip(t, axis=1)
    y_bwd = fl(ssd_scan(fl(xs), fl(dt[:, :, 1]), a[1], fl(bs), fl(cs)))
    y = y_fwd + y_bwd + xs * d_skip.astype(jnp.float32).reshape(G, R, 1)
    y = gated_rmsnorm(y.reshape(Bsz, S, D_INNER), z, norm_w)
    return y.astype(z.dtype) @ w_out


def t5_bucket(rel):
    nb = N_BUCKETS // 2
    max_exact = nb // 2
    ret = jnp.where(rel > 0, nb, 0)
    n = jnp.abs(rel)
    nf = jnp.maximum(n, 1).astype(jnp.float32)
    large = max_exact + (jnp.log(nf / max_exact) / math.log(MAX_DISTANCE / max_exact)
                         * (nb - max_exact)).astype(jnp.int32)
    large = jnp.minimum(large, nb - 1)
    return ret + jnp.where(n < max_exact, n, large)


def window_attention(q, k, v, sink, rel_table):
    Bsz, S = q.shape[:2]
    nb = S // BLOCK
    rep = ATTN_Q_HEADS // ATTN_KV_HEADS
    qb = q.reshape(Bsz, nb, BLOCK, ATTN_KV_HEADS, rep, ATTN_HEAD_DIM)

    def band(t):
        t = t.reshape(Bsz, S, ATTN_KV_HEADS, ATTN_HEAD_DIM)
        t = jnp.pad(t, ((0, 0), (BLOCK, BLOCK), (0, 0), (0, 0)))
        t = t.reshape(Bsz, nb + 2, BLOCK, ATTN_KV_HEADS, ATTN_HEAD_DIM)
        return jnp.concatenate([t[:, :-2], t[:, 1:-1], t[:, 2:]], axis=2)

    kb, vb = band(k), band(v)
    logits = jnp.einsum('bnqgrd,bnkgd->bngrqk', qb, kb).astype(jnp.float32) * (ATTN_HEAD_DIM ** -0.5)
    i = jnp.arange(BLOCK)[:, None]
    j = jnp.arange(3 * BLOCK)[None, :]
    rel = j - BLOCK - i
    bias = rel_table[t5_bucket(rel)].astype(jnp.float32)
    bias = jnp.transpose(bias, (2, 0, 1)).reshape(ATTN_KV_HEADS, rep, BLOCK, 3 * BLOCK)
    kpos = jnp.arange(nb)[:, None] * BLOCK + j - BLOCK
    valid = (jnp.abs(rel) <= WINDOW)[None] & ((kpos >= 0) & (kpos < S))[:, None, :]
    logits = jnp.where(valid[None, :, None, None], logits + bias, -jnp.inf)
    sink_l = sink.astype(jnp.float32).reshape(1, 1, ATTN_KV_HEADS, rep, 1, 1)
    m = jnp.maximum(jnp.max(logits, axis=-1, keepdims=True), sink_l)
    p = jnp.exp(logits - m)
    p = p / (jnp.sum(p, axis=-1, keepdims=True) + jnp.exp(sink_l - m))
    out = jnp.einsum('bngrqk,bnkgd->bnqgrd', p.astype(v.dtype), vb)
    return out.reshape(Bsz, S, ATTN_WIDTH)


def setup_inputs(seed: int = 0) -> dict:
    key = jax.random.key(seed)
    ks = jax.random.split(key, 24)
    f32 = jnp.float32
    nrm = lambda k, shape, s: jax.random.normal(k, shape, f32) * s
    gain = lambda k, shape: 1.0 + 0.05 * jax.random.normal(k, shape, f32)
    dt0 = jnp.exp(jax.random.uniform(ks[10], (DEPTH, 2, SSD_HEADS), f32, math.log(1e-3), math.log(1e-1)))
    return {
        'x': jax.random.normal(ks[0], (BATCH, SEQ, D_MODEL), f32),
        'pre_mix_norm': gain(ks[1], (DEPTH, D_MODEL)),
        'w_in': nrm(ks[2], (DEPTH, D_MODEL, N_IN), D_MODEL ** -0.5),
        'b_gate': nrm(ks[3], (DEPTH, 2 * D_MODEL), 0.1),
        'conv_w': nrm(ks[4], (DEPTH, SSD_CONV, 1, CONV_DIM), SSD_CONV ** -0.5),
        'conv_b': nrm(ks[5], (DEPTH, CONV_DIM), 0.02),
        'dt_bias': dt0 + jnp.log(-jnp.expm1(-dt0)),
        'a_log': jnp.log(jax.random.uniform(ks[6], (DEPTH, 2, SSD_HEADS), f32, 1.0, 16.0)),
        'd_skip': gain(ks[7], (DEPTH, SSD_HEADS)),
        'ssd_norm': gain(ks[8], (DEPTH, D_INNER)),
        'w_ssd_out': nrm(ks[9], (DEPTH, D_INNER, D_MODEL), D_INNER ** -0.5),
        'attn_sink': nrm(ks[11], (DEPTH, ATTN_Q_HEADS), 0.5),
        'rel_bias_table': nrm(ks[12], (N_BUCKETS, ATTN_Q_HEADS), 0.5),
        'w_attn_out': nrm(ks[13], (DEPTH, ATTN_WIDTH, D_MODEL), ATTN_WIDTH ** -0.5),
        'w_o': nrm(ks[14], (DEPTH, D_MODEL, D_MODEL), D_MODEL ** -0.5),
        'post_mix_norm': gain(ks[15], (DEPTH, D_MODEL)),
        'pre_mlp_norm': gain(ks[16], (DEPTH, D_MODEL)),
        'w_mlp_in': nrm(ks[17], (DEPTH, D_MODEL, D_FF), D_MODEL ** -0.5),
        'w_mlp_out': nrm(ks[18], (DEPTH, D_FF, D_MODEL), D_FF ** -0.5),
        'post_mlp_norm': gain(ks[19], (DEPTH, D_MODEL)),
    }


def reference(x, pre_mix_norm, w_in, b_gate, conv_w, conv_b, dt_bias, a_log, d_skip, ssd_norm,
              w_ssd_out, attn_sink, rel_bias_table, w_attn_out, w_o, post_mix_norm,
              pre_mlp_norm, w_mlp_in, w_mlp_out, post_mlp_norm):
    for l in range(DEPTH):
        h = rmsnorm(x, pre_mix_norm[l])
        proj = h @ w_in[l]
        z, xbc, dt_raw, q, k, v, gates = split_cols(proj, IN_SPLITS)
        y_ssd = ssd_branch(z, xbc, dt_raw, conv_w[l], conv_b[l], dt_bias[l], a_log[l],
                           d_skip[l], ssd_norm[l], w_ssd_out[l])
        y_attn = window_attention(q, k, v, attn_sink[l], rel_bias_table) @ w_attn_out[l]
        g = jax.nn.sigmoid((gates + b_gate[l]).astype(jnp.float32)).astype(x.dtype)
        g_ssd, g_attn = g[..., :D_MODEL], g[..., D_MODEL:]
        mixed = (g_ssd * y_ssd + g_attn * y_attn) @ w_o[l]
        x = x + rmsnorm(mixed, post_mix_norm[l])
        h = rmsnorm(x, pre_mlp_norm[l])
        f = jnp.square(jax.nn.relu(h @ w_mlp_in[l])) @ w_mlp_out[l]
        x = x + rmsnorm(f, post_mlp_norm[l])
    return x
```

```python
import functools
import math

import numpy as np
import jax
import jax.numpy as jnp
from jax import lax
from jax.experimental import pallas as pl
from jax.experimental.pallas import tpu as pltpu

F32 = jnp.float32
BF16 = jnp.bfloat16
EPS = 1e-6

D_MODEL = 1024
D_INNER = 2048
SSD_HEADS = 32
SSD_HEAD_DIM = 64
SSD_GROUPS = 8
HEADS_PER_GROUP = SSD_HEADS // SSD_GROUPS
GROUP_WIDTH = D_INNER // SSD_GROUPS
SSD_STATE = 128
CHUNK = 128
CONV_TAPS = 5
CONV_DIM = D_INNER + 2 * SSD_GROUPS * SSD_STATE
Q_HEADS = 16
KV_HEADS = 4
HEAD_DIM = 64
ATTN_WIDTH = Q_HEADS * HEAD_DIM
KV_WIDTH = KV_HEADS * HEAD_DIM
WINDOW = 128
ATTN_BLOCK = 128
N_BUCKETS = 32
MAX_DISTANCE = 128
D_FF = 4 * D_MODEL
DT_PAD = 128
LANES = 128
HALO_ROWS = 8

COL_XBC = 0
COL_Z = CONV_DIM
COL_GATES = COL_Z + D_INNER
COL_Q = COL_GATES + 2 * D_MODEL
COL_K = COL_Q + ATTN_WIDTH
COL_V = COL_K + KV_WIDTH
COL_DT = COL_V + KV_WIDTH
N_PROJ = COL_DT + DT_PAD

VMEM_LIMIT = 56 * 1024 * 1024


def _params(n_axes=1):
    return pltpu.CompilerParams(dimension_semantics=("arbitrary",) * n_axes,
                                vmem_limit_bytes=VMEM_LIMIT)


def _resident(shape):
    nd = len(shape)
    return pl.BlockSpec(shape, lambda *_: (0,) * nd, pipeline_mode=pl.Buffered(1))


def _rms(x, g):
    ms = jnp.mean(x * x, axis=-1, keepdims=True)
    return x * lax.rsqrt(ms + EPS) * g


def _sigmoid(x):
    return 1.0 / (1.0 + jnp.exp(-x))


def _softplus(x):
    return jnp.maximum(x, 0.0) + jnp.log1p(jnp.exp(-jnp.abs(x)))


def _inproj_kernel(x_ref, g_ref, w_ref, xbc_ref, zg_ref, qkv_ref, dt_ref):
    h = _rms(x_ref[...], g_ref[...]).astype(BF16)

    def mm(lo, hi):
        return jnp.dot(h, w_ref[:, lo:hi], preferred_element_type=F32)

    step = 1024
    for j in range(CONV_DIM // step):
        xbc_ref[:, j * step:(j + 1) * step] = mm(COL_XBC + j * step, COL_XBC + (j + 1) * step)
    for j in range((COL_Q - COL_Z) // step):
        zg_ref[:, j * step:(j + 1) * step] = mm(COL_Z + j * step, COL_Z + (j + 1) * step)
    qkv_ref[...] = mm(COL_Q, COL_DT).astype(BF16)
    dt_ref[...] = mm(COL_DT, N_PROJ)


def _inproj(x, g, w, tm=256):
    T = x.shape[0]
    row = lambda i: (i, 0)
    return pl.pallas_call(
        _inproj_kernel,
        grid=(T // tm,),
        in_specs=[pl.BlockSpec((tm, D_MODEL), row), _resident((1, D_MODEL)), _resident((D_MODEL, N_PROJ))],
        out_specs=[pl.BlockSpec((tm, CONV_DIM), row), pl.BlockSpec((tm, COL_Q - COL_Z), row),
                   pl.BlockSpec((tm, COL_DT - COL_Q), row), pl.BlockSpec((tm, DT_PAD), row)],
        out_shape=[jax.ShapeDtypeStruct((T, CONV_DIM), F32), jax.ShapeDtypeStruct((T, COL_Q - COL_Z), F32),
                   jax.ShapeDtypeStruct((T, COL_DT - COL_Q), BF16), jax.ShapeDtypeStruct((T, DT_PAD), F32)],
        compiler_params=_params(),
        name="inproj",
    )(x, g, w)


def _conv_kernel(prev_ref, cur_ref, next_ref, w_ref, b_ref, xs_ref, bc_ref, *, steps_per_seq, ts):
    i = pl.program_id(0)
    first = (i % steps_per_seq) == 0
    last = (i % steps_per_seq) == steps_per_seq - 1
    cw = 512
    n = ts + 2 * HALO_ROWS
    for cc in range(CONV_DIM // cw):
        sl = slice(cc * cw, (cc + 1) * cw)
        pv = jnp.where(first, 0.0, prev_ref[:, sl])
        nx = jnp.where(last, 0.0, next_ref[:, sl])
        ext = jnp.concatenate([pv, cur_ref[:, sl], nx], axis=0)
        acc = None
        for k in range(CONV_TAPS):
            sh = (CONV_TAPS // 2 - k) % n
            r = ext if sh == 0 else pltpu.roll(ext, sh, axis=0)
            term = r[HALO_ROWS:HALO_ROWS + ts, :] * w_ref[k:k + 1, sl]
            acc = term if acc is None else acc + term
        acc = acc + b_ref[:, sl]
        y = acc * _sigmoid(acc)
        if cc * cw < D_INNER:
            xs_ref[:, sl] = y
        else:
            bc_ref[:, cc * cw - D_INNER:(cc + 1) * cw - D_INNER] = y.astype(BF16)


def _conv(xbc, w, b, seq, ts=128):
    T = xbc.shape[0]
    hb = ts // HALO_ROWS
    last_halo = T // HALO_ROWS - 1
    row = lambda i: (i, 0)
    kern = functools.partial(_conv_kernel, steps_per_seq=seq // ts, ts=ts)
    return pl.pallas_call(
        kern,
        grid=(T // ts,),
        in_specs=[pl.BlockSpec((HALO_ROWS, CONV_DIM), lambda i: (jnp.maximum(i * hb - 1, 0), 0)),
                  pl.BlockSpec((ts, CONV_DIM), row),
                  pl.BlockSpec((HALO_ROWS, CONV_DIM), lambda i: (jnp.minimum((i + 1) * hb, last_halo), 0)),
                  _resident((CONV_TAPS, CONV_DIM)), _resident((1, CONV_DIM))],
        out_specs=[pl.BlockSpec((ts, D_INNER), row), pl.BlockSpec((ts, CONV_DIM - D_INNER), row)],
        out_shape=[jax.ShapeDtypeStruct((T, D_INNER), F32), jax.ShapeDtypeStruct((T, CONV_DIM - D_INNER), BF16)],
        compiler_params=_params(),
        name="conv_silu",
    )(xbc, xbc, xbc, w, b)


def _split3(v):
    hi = v.astype(BF16)
    r1 = v - hi.astype(F32)
    mid = r1.astype(BF16)
    lo = (r1 - mid.astype(F32)).astype(BF16)
    return jnp.concatenate([hi, mid, lo], axis=1)


def _expand_heads(v, e3_ref, first_col):
    lane = lax.broadcasted_iota(jnp.int32, v.shape, 1)
    keep = jnp.logical_and(lane >= first_col, lane < first_col + SSD_HEADS)
    return jnp.dot(_split3(jnp.where(keep, v, 0.0)), e3_ref[...], preferred_element_type=F32)


def _ssd_decays(dt_ref, dtb_ref, alog_ref):
    dt = _softplus(dt_ref[...] + dtb_ref[...])
    adt = dt * (-jnp.exp(alog_ref[...]))
    row = lax.broadcasted_iota(jnp.int32, (CHUNK, CHUNK), 0)
    col = lax.broadcasted_iota(jnp.int32, (CHUNK, CHUNK), 1)
    tril = (row >= col).astype(F32)
    triu = (row <= col).astype(F32)
    acs_f = jnp.dot(tril, adt, precision=lax.Precision.HIGHEST, preferred_element_type=F32)
    acs_b = jnp.dot(triu, adt, precision=lax.Precision.HIGHEST, preferred_element_type=F32)
    lane = lax.broadcasted_iota(jnp.int32, (CHUNK, DT_PAD), 1)
    acs = jnp.where(lane < SSD_HEADS, acs_f, acs_b)
    return dt, acs, row, col


_NT = (((1,), (1,)), ((), ()))
_TN = (((0,), (0,)), ((), ()))


def _ssd_fwd_kernel(xs_ref, bc_ref, dt_ref, dtb_ref, alog_ref, e3_ref, y_ref, s_ref, *, nc):
    c = pl.program_id(0)

    @pl.when(c % nc == 0)
    def _():
        s_ref[...] = jnp.zeros_like(s_ref)

    dt, acs, row, col = _ssd_decays(dt_ref, dtb_ref, alog_ref)
    acs_t = acs.T
    dt_t = dt.T
    decay_in = _expand_heads(jnp.exp(acs), e3_ref, 0)
    w_state = _expand_heads(jnp.exp(acs[CHUNK - 1:CHUNK, :] - acs) * dt, e3_ref, 0)
    xs = xs_ref[...]
    xw = (xs * w_state).astype(BF16)
    xb = xs.astype(BF16)
    lower = row >= col
    strict_lower = row > col
    strict_upper = row < col
    head_of_lane = lax.broadcasted_iota(jnp.int32, (CHUNK, GROUP_WIDTH), 1) // SSD_HEAD_DIM
    nb = SSD_GROUPS * SSD_STATE
    for g in range(SSD_GROUPS):
        gs = slice(g * GROUP_WIDTH, (g + 1) * GROUP_WIDTH)
        b_g = bc_ref[:, g * SSD_STATE:(g + 1) * SSD_STATE]
        c_g = bc_ref[:, nb + g * SSD_STATE:nb + (g + 1) * SSD_STATE]
        cb = lax.dot_general(c_g, b_g, _NT, preferred_element_type=F32)
        x_g = xb[:, gs]
        y_diag = None
        for r in range(HEADS_PER_GROUP):
            hf = g * HEADS_PER_GROUP + r
            hb = SSD_HEADS + hf
            diff = jnp.where(lower, acs[:, hf:hf + 1] - acs_t[hf:hf + 1, :],
                             acs[:, hb:hb + 1] - acs_t[hb:hb + 1, :])
            dt_f = dt_t[hf:hf + 1, :]
            dt_b = dt_t[hb:hb + 1, :]
            w = jnp.where(strict_lower, dt_f, jnp.where(strict_upper, dt_b, dt_f + dt_b))
            m = (cb * jnp.exp(diff) * w).astype(BF16)
            x_h = jnp.where(head_of_lane == r, x_g, jnp.zeros_like(x_g))
            t = jnp.dot(m, x_h, preferred_element_type=F32)
            y_diag = t if y_diag is None else y_diag + t
        s_g = s_ref[g]
        y_off = jnp.dot(c_g, s_g.astype(BF16), preferred_element_type=F32) * decay_in[:, gs]
        y_ref[:, gs] = y_diag + y_off
        upd = lax.dot_general(b_g, xw[:, gs], _TN, preferred_element_type=F32)
        s_ref[g] = s_g * decay_in[CHUNK - 1:CHUNK, gs] + upd


def _ssd_fwd(xs, bc, dt, dtb, alog, e3, seq):
    T = xs.shape[0]
    row = lambda c: (c, 0)
    kern = functools.partial(_ssd_fwd_kernel, nc=seq // CHUNK)
    return pl.pallas_call(
        kern,
        grid=(T // CHUNK,),
        in_specs=[pl.BlockSpec((CHUNK, D_INNER), row), pl.BlockSpec((CHUNK, CONV_DIM - D_INNER), row),
                  pl.BlockSpec((CHUNK, DT_PAD), row), _resident((1, DT_PAD)), _resident((1, DT_PAD)),
                  _resident(e3.shape)],
        out_specs=pl.BlockSpec((CHUNK, D_INNER), row),
        out_shape=jax.ShapeDtypeStruct((T, D_INNER), F32),
        scratch_shapes=[pltpu.VMEM((SSD_GROUPS, SSD_STATE, GROUP_WIDTH), F32)],
        compiler_params=_params(),
        name="ssd_fwd",
    )(xs, bc, dt, dtb, alog, e3)


def _ssd_bwd_kernel(xs_ref, bc_ref, dt_ref, dtb_ref, alog_ref, e3_ref, yp_ref, z_ref, gs_ref, ga_ref, x_ref,
                    dskip_ref, nw_ref, wso_ref, bg_ref, wo_ref, pmn_ref, out_ref, s_ref, *, nc):
    c = pl.program_id(0)

    @pl.when(c % nc == 0)
    def _():
        s_ref[...] = jnp.zeros_like(s_ref)

    dt, acs, _, _ = _ssd_decays(dt_ref, dtb_ref, alog_ref)
    decay_in = _expand_heads(jnp.exp(acs), e3_ref, SSD_HEADS)
    w_state = _expand_heads(jnp.exp(acs[0:1, :] - acs) * dt, e3_ref, SSD_HEADS)
    xs = xs_ref[...]
    xw = (xs * w_state).astype(BF16)
    nb = SSD_GROUPS * SSD_STATE
    normed = []
    for g in range(SSD_GROUPS):
        gs = slice(g * GROUP_WIDTH, (g + 1) * GROUP_WIDTH)
        b_g = bc_ref[:, g * SSD_STATE:(g + 1) * SSD_STATE]
        c_g = bc_ref[:, nb + g * SSD_STATE:nb + (g + 1) * SSD_STATE]
        s_g = s_ref[g]
        y_off = jnp.dot(c_g, s_g.astype(BF16), preferred_element_type=F32) * decay_in[:, gs]
        y = yp_ref[:, gs] + y_off + xs[:, gs] * dskip_ref[:, gs]
        z = z_ref[:, gs]
        u = y * (z * _sigmoid(z))
        normed.append((_rms(u, nw_ref[:, gs])).astype(BF16))
        upd = lax.dot_general(b_g, xw[:, gs], _TN, preferred_element_type=F32)
        s_ref[g] = s_g * decay_in[0:1, gs] + upd
    yn = jnp.concatenate(normed, axis=1)
    y_ssd = jnp.dot(yn, wso_ref[...], preferred_element_type=F32)
    gate = _sigmoid(gs_ref[...] + bg_ref[...])
    mix = (gate * y_ssd + ga_ref[...]).astype(BF16)
    mixed = jnp.dot(mix, wo_ref[...], preferred_element_type=F32)
    out_ref[...] = x_ref[...] + _rms(mixed, pmn_ref[...])


def _ssd_bwd(xs, bc, dt, dtb, alog, e3, yp, zg, ga, x, dskip, nw, wso, bg, wo, pmn, seq):
    T = xs.shape[0]
    last = T // CHUNK - 1
    rev = lambda c: (last - c, 0)
    kern = functools.partial(_ssd_bwd_kernel, nc=seq // CHUNK)
    return pl.pallas_call(
        kern,
        grid=(T // CHUNK,),
        in_specs=[pl.BlockSpec((CHUNK, D_INNER), rev), pl.BlockSpec((CHUNK, CONV_DIM - D_INNER), rev),
                  pl.BlockSpec((CHUNK, DT_PAD), rev), _resident((1, DT_PAD)), _resident((1, DT_PAD)),
                  _resident(e3.shape),
                  pl.BlockSpec((CHUNK, D_INNER), rev),
                  pl.BlockSpec((CHUNK, D_INNER), rev),
                  pl.BlockSpec((CHUNK, D_MODEL), lambda c: (last - c, 2)),
                  pl.BlockSpec((CHUNK, D_MODEL), rev), pl.BlockSpec((CHUNK, D_MODEL), rev),
                  _resident((1, D_INNER)), _resident((1, D_INNER)), _resident((D_INNER, D_MODEL)),
                  _resident((1, D_MODEL)), _resident((D_MODEL, D_MODEL)), _resident((1, D_MODEL))],
        out_specs=pl.BlockSpec((CHUNK, D_MODEL), rev),
        out_shape=jax.ShapeDtypeStruct((T, D_MODEL), F32),
        scratch_shapes=[pltpu.VMEM((SSD_GROUPS, SSD_STATE, GROUP_WIDTH), F32)],
        compiler_params=_params(),
        name="ssd_bwd_mix",
    )(xs, bc, dt, dtb, alog, e3, yp, zg, zg, ga, x, dskip, nw, wso, bg, wo, pmn)


def _attn_kernel(sink_ref, q_ref, kp_ref, kc_ref, kn_ref, vp_ref, vc_ref, vn_ref, bias_ref, gate_ref, bg_ref,
                 wao_ref, out_ref, *, nb):
    n = pl.program_id(0)
    first = (n % nb) == 0
    last = (n % nb) == nb - 1
    kcol = lax.broadcasted_iota(jnp.int32, (1, 3 * ATTN_BLOCK), 1)
    in_seq = jnp.logical_and(jnp.logical_or(kcol >= ATTN_BLOCK, jnp.logical_not(first)),
                             jnp.logical_or(kcol < 2 * ATTN_BLOCK, jnp.logical_not(last)))
    k = jnp.concatenate([kp_ref[...], kc_ref[...], kn_ref[...]], axis=0)
    v = jnp.concatenate([vp_ref[...], vc_ref[...], vn_ref[...]], axis=0)
    low_half = lax.broadcasted_iota(jnp.int32, (ATTN_BLOCK, LANES), 1) < HEAD_DIM
    scale = HEAD_DIM ** -0.5
    outs = []
    for t in range(KV_HEADS // 2):
        kt = k[:, t * LANES:(t + 1) * LANES]
        vt = v[:, t * LANES:(t + 1) * LANES]
        for r in range(4):
            j = 4 * t + r
            qt = q_ref[:, j * LANES:(j + 1) * LANES]
            halves = []
            for e in range(2):
                head = 8 * t + 4 * e + r
                keep = low_half if e == 0 else jnp.logical_not(low_half)
                qz = jnp.where(keep, qt, jnp.zeros_like(qt))
                s = lax.dot_general(qz, kt, _NT, preferred_element_type=F32) * scale + bias_ref[head]
                s = jnp.where(in_seq, s, -jnp.inf)
                sk = sink_ref[head]
                m = jnp.maximum(jnp.max(s, axis=-1, keepdims=True), sk)
                p = jnp.exp(s - m)
                den = jnp.sum(p, axis=-1, keepdims=True) + jnp.exp(sk - m)
                p = (p / den).astype(BF16)
                halves.append(jnp.dot(p, vt, preferred_element_type=F32))
            outs.append(jnp.where(low_half, halves[0], halves[1]).astype(BF16))
    o = jnp.concatenate(outs, axis=1)
    y = jnp.dot(o, wao_ref[...], preferred_element_type=F32)
    out_ref[...] = _sigmoid(gate_ref[...] + bg_ref[...]) * y


def _attn(sink, qkv, bias, zg, bg, wao, seq):
    T = qkv.shape[0]
    nb = seq // ATTN_BLOCK
    nblk = T // ATTN_BLOCK
    kblk = COL_K - COL_Q
    kcol = kblk // KV_WIDTH
    vcol = kcol + 1
    prev = lambda n: jnp.maximum(n - 1, 0)
    nxt = lambda n: jnp.minimum(n + 1, nblk - 1)
    kern = functools.partial(_attn_kernel, nb=nb)
    return pl.pallas_call(
        kern,
        grid=(nblk,),
        in_specs=[pl.BlockSpec(memory_space=pltpu.SMEM),
                  pl.BlockSpec((ATTN_BLOCK, ATTN_WIDTH), lambda n: (n, 0)),
                  pl.BlockSpec((ATTN_BLOCK, KV_WIDTH), lambda n: (prev(n), kcol)),
                  pl.BlockSpec((ATTN_BLOCK, KV_WIDTH), lambda n: (n, kcol)),
                  pl.BlockSpec((ATTN_BLOCK, KV_WIDTH), lambda n: (nxt(n), kcol)),
                  pl.BlockSpec((ATTN_BLOCK, KV_WIDTH), lambda n: (prev(n), vcol)),
                  pl.BlockSpec((ATTN_BLOCK, KV_WIDTH), lambda n: (n, vcol)),
                  pl.BlockSpec((ATTN_BLOCK, KV_WIDTH), lambda n: (nxt(n), vcol)),
                  _resident(bias.shape),
                  pl.BlockSpec((ATTN_BLOCK, D_MODEL), lambda n: (n, 3)),
                  _resident((1, D_MODEL)), _resident((ATTN_WIDTH, D_MODEL))],
        out_specs=pl.BlockSpec((ATTN_BLOCK, D_MODEL), lambda n: (n, 0)),
        out_shape=jax.ShapeDtypeStruct((T, D_MODEL), F32),
        compiler_params=_params(),
        name="attn",
    )(sink, qkv, qkv, qkv, qkv, qkv, qkv, qkv, bias, zg, bg, wao)


def _mlp_kernel(x_ref, g1_ref, w1_ref, w2_ref, g2_ref, out_ref):
    x = x_ref[...]
    h = _rms(x, g1_ref[...]).astype(BF16)
    step = 1024
    acc = None
    for j in range(D_FF // step):
        a = jnp.dot(h, w1_ref[:, j * step:(j + 1) * step], preferred_element_type=F32)
        a = jnp.square(jnp.maximum(a, 0.0)).astype(BF16)
        t = jnp.dot(a, w2_ref[j * step:(j + 1) * step, :], preferred_element_type=F32)
        acc = t if acc is None else acc + t
    out_ref[...] = x + _rms(acc, g2_ref[...])


def _mlp(x, g1, w1, w2, g2, tm=256):
    T = x.shape[0]
    row = lambda i: (i, 0)
    return pl.pallas_call(
        _mlp_kernel,
        grid=(T // tm,),
        in_specs=[pl.BlockSpec((tm, D_MODEL), row), _resident((1, D_MODEL)), _resident((D_MODEL, D_FF)),
                  _resident((D_FF, D_MODEL)), _resident((1, D_MODEL))],
        out_specs=pl.BlockSpec((tm, D_MODEL), row),
        out_shape=jax.ShapeDtypeStruct((T, D_MODEL), F32),
        compiler_params=_params(),
        name="mlp",
    )(x, g1, w1, w2, g2)


def _q_head_order():
    order = []
    for t in range(KV_HEADS // 2):
        for r in range(4):
            order += [8 * t + r, 8 * t + 4 + r]
    return order


def _head_cols(order):
    return np.concatenate([np.arange(h * HEAD_DIM, (h + 1) * HEAD_DIM) for h in order])


def _prep_w_in(w):
    z0, xbc0 = 0, D_INNER
    dt0 = xbc0 + CONV_DIM
    q0 = dt0 + 2 * SSD_HEADS
    k0 = q0 + ATTN_WIDTH
    v0 = k0 + KV_WIDTH
    g0 = v0 + KV_WIDTH
    qcols = q0 + _head_cols(_q_head_order())
    pad = jnp.zeros((D_MODEL, DT_PAD - 2 * SSD_HEADS), w.dtype)
    parts = [w[:, xbc0:dt0], w[:, z0:xbc0], w[:, g0:g0 + 2 * D_MODEL], w[:, qcols], w[:, k0:v0], w[:, v0:g0],
             w[:, dt0:q0], pad]
    return jnp.concatenate(parts, axis=1).astype(BF16)


def _expand_matrix(first_col):
    e = np.zeros((3 * DT_PAD, D_INNER), np.float32)
    for p in range(3):
        for h in range(SSD_HEADS):
            e[p * DT_PAD + first_col + h, h * SSD_HEAD_DIM:(h + 1) * SSD_HEAD_DIM] = 1.0
    return jnp.asarray(e, BF16)


def _t5_bucket(rel):
    nb = N_BUCKETS // 2
    max_exact = nb // 2
    ret = jnp.where(rel > 0, nb, 0)
    n = jnp.abs(rel)
    nf = jnp.maximum(n, 1).astype(F32)
    large = max_exact + (jnp.log(nf / max_exact) / math.log(MAX_DISTANCE / max_exact)
                         * (nb - max_exact)).astype(jnp.int32)
    large = jnp.minimum(large, nb - 1)
    return ret + jnp.where(n < max_exact, n, large)


def _window_bias(rel_table):
    i = jnp.arange(ATTN_BLOCK)[:, None]
    j = jnp.arange(3 * ATTN_BLOCK)[None, :]
    rel = j - ATTN_BLOCK - i
    bias = rel_table[_t5_bucket(rel)].astype(F32)
    bias = jnp.where((jnp.abs(rel) <= WINDOW)[..., None], bias, -jnp.inf)
    return jnp.transpose(bias, (2, 0, 1))


def _pad_lanes(v):
    return jnp.pad(v.reshape(1, -1).astype(F32), ((0, 0), (0, DT_PAD - v.size)))


def kernel(x, pre_mix_norm, w_in, b_gate, conv_w, conv_b, dt_bias, a_log, d_skip, ssd_norm, w_ssd_out, attn_sink,
           rel_bias_table, w_attn_out, w_o, post_mix_norm, pre_mlp_norm, w_mlp_in, w_mlp_out, post_mlp_norm):
    bsz, seq, d = x.shape
    assert d == D_MODEL and seq % CHUNK == 0 and seq % ATTN_BLOCK == 0
    depth = w_in.shape[0]
    xf = x.reshape(bsz * seq, d)
    e3_fwd = _expand_matrix(0)
    e3_bwd = _expand_matrix(SSD_HEADS)
    bias = _window_bias(rel_bias_table)
    ao_rows = _head_cols(_q_head_order())
    r1 = lambda v: v.reshape(1, -1)
    for l in range(depth):
        xbc, zg, qkv, dtr = _inproj(xf, r1(pre_mix_norm[l]), _prep_w_in(w_in[l]))
        xs, bc = _conv(xbc, conv_w[l].reshape(CONV_TAPS, CONV_DIM), r1(conv_b[l]), seq)
        dtb = _pad_lanes(dt_bias[l])
        alog = _pad_lanes(a_log[l])
        yp = _ssd_fwd(xs, bc, dtr, dtb, alog, e3_fwd, seq)
        ga = _attn(attn_sink[l], qkv, bias, zg, r1(b_gate[l, D_MODEL:]), w_attn_out[l][ao_rows].astype(BF16), seq)
        dskip = r1(jnp.repeat(d_skip[l], SSD_HEAD_DIM))
        xf = _ssd_bwd(xs, bc, dtr, dtb, alog, e3_bwd, yp, zg, ga, xf, dskip, r1(ssd_norm[l]),
                      w_ssd_out[l].astype(BF16), r1(b_gate[l, :D_MODEL]), w_o[l].astype(BF16),
                      r1(post_mix_norm[l]), seq)
        xf = _mlp(xf, r1(pre_mlp_norm[l]), w_mlp_in[l].astype(BF16), w_mlp_out[l].astype(BF16),
                  r1(post_mlp_norm[l]))
    return xf.reshape(bsz, seq, d)
```

```python
import functools
import math

import numpy as np
import jax
import jax.numpy as jnp
from jax import lax
from jax.experimental import pallas as pl
from jax.experimental.pallas import tpu as pltpu

F32 = jnp.float32
BF16 = jnp.bfloat16
EPS = 1e-6

D_MODEL = 1024
D_INNER = 2048
SSD_HEADS = 32
SSD_HEAD_DIM = 64
SSD_GROUPS = 8
HEADS_PER_GROUP = SSD_HEADS // SSD_GROUPS
GROUP_WIDTH = D_INNER // SSD_GROUPS
SSD_STATE = 128
CHUNK = 128
CONV_TAPS = 5
CONV_DIM = D_INNER + 2 * SSD_GROUPS * SSD_STATE
Q_HEADS = 16
KV_HEADS = 4
HEAD_DIM = 64
ATTN_WIDTH = Q_HEADS * HEAD_DIM
KV_WIDTH = KV_HEADS * HEAD_DIM
WINDOW = 128
ATTN_BLOCK = 128
N_BUCKETS = 32
MAX_DISTANCE = 128
D_FF = 4 * D_MODEL
DT_PAD = 128
LANES = 128
HALO_ROWS = 8

ZG_WIDTH = D_INNER + 2 * D_MODEL
QKV_WIDTH = ATTN_WIDTH + 2 * KV_WIDTH

VMEM_LIMIT = 56 * 1024 * 1024


def _params(n_axes=1):
    return pltpu.CompilerParams(dimension_semantics=("arbitrary",) * n_axes,
                                vmem_limit_bytes=VMEM_LIMIT)


def _resident(shape):
    nd = len(shape)
    return pl.BlockSpec(shape, lambda *_: (0,) * nd, pipeline_mode=pl.Buffered(1))


def _rms(x, g):
    ms = jnp.mean(x * x, axis=-1, keepdims=True)
    return x * lax.rsqrt(ms + EPS) * g


def _sigmoid(x):
    return 1.0 / (1.0 + jnp.exp(-x))


def _softplus(x):
    return jnp.maximum(x, 0.0) + jnp.log1p(jnp.exp(-jnp.abs(x)))


def _inproj_kernel(x_ref, g_ref, wzx_ref, wq_ref, wkvg_ref, wdt_ref, xbc_ref, zg_ref, qkv_ref, dt_ref):
    h = _rms(x_ref[...], g_ref[...]).astype(BF16)

    def mm(w_ref, lo, hi):
        return jnp.dot(h, w_ref[:, lo:hi], preferred_element_type=F32)

    step = 1024
    for j in range(CONV_DIM // step):
        xbc_ref[:, j * step:(j + 1) * step] = mm(wzx_ref, D_INNER + j * step, D_INNER + (j + 1) * step)
    for j in range(D_INNER // step):
        zg_ref[:, j * step:(j + 1) * step] = mm(wzx_ref, j * step, (j + 1) * step)
    for j in range(2 * D_MODEL // step):
        zg_ref[:, D_INNER + j * step:D_INNER + (j + 1) * step] = mm(
            wkvg_ref, 2 * KV_WIDTH + j * step, 2 * KV_WIDTH + (j + 1) * step)
    qkv_ref[:, :ATTN_WIDTH] = mm(wq_ref, 0, ATTN_WIDTH).astype(BF16)
    qkv_ref[:, ATTN_WIDTH:] = mm(wkvg_ref, 0, 2 * KV_WIDTH).astype(BF16)
    dt_ref[...] = mm(wdt_ref, 0, DT_PAD)


def _inproj(x, g, wzx, wq, wkvg, wdt, tm=256):
    T = x.shape[0]
    row = lambda i: (i, 0)
    return pl.pallas_call(
        _inproj_kernel,
        grid=(T // tm,),
        in_specs=[pl.BlockSpec((tm, D_MODEL), row), _resident((1, D_MODEL)), _resident(wzx.shape),
                  _resident(wq.shape), _resident(wkvg.shape), _resident(wdt.shape)],
        out_specs=[pl.BlockSpec((tm, CONV_DIM), row), pl.BlockSpec((tm, ZG_WIDTH), row),
                   pl.BlockSpec((tm, QKV_WIDTH), row), pl.BlockSpec((tm, DT_PAD), row)],
        out_shape=[jax.ShapeDtypeStruct((T, CONV_DIM), F32), jax.ShapeDtypeStruct((T, ZG_WIDTH), F32),
                   jax.ShapeDtypeStruct((T, QKV_WIDTH), BF16), jax.ShapeDtypeStruct((T, DT_PAD), F32)],
        compiler_params=_params(),
        name="inproj",
    )(x, g, wzx, wq, wkvg, wdt)


def _conv_kernel(prev_ref, cur_ref, next_ref, w_ref, b_ref, xs_ref, bc_ref, *, steps_per_seq, ts):
    i = pl.program_id(0)
    first = (i % steps_per_seq) == 0
    last = (i % steps_per_seq) == steps_per_seq - 1
    cw = 512
    n = ts + 2 * HALO_ROWS
    for cc in range(CONV_DIM // cw):
        sl = slice(cc * cw, (cc + 1) * cw)
        pv = jnp.where(first, 0.0, prev_ref[:, sl])
        nx = jnp.where(last, 0.0, next_ref[:, sl])
        ext = jnp.concatenate([pv, cur_ref[:, sl], nx], axis=0)
        acc = None
        for k in range(CONV_TAPS):
            sh = (CONV_TAPS // 2 - k) % n
            r = ext if sh == 0 else pltpu.roll(ext, sh, axis=0)
            term = r[HALO_ROWS:HALO_ROWS + ts, :] * w_ref[k:k + 1, sl]
            acc = term if acc is None else acc + term
        acc = acc + b_ref[:, sl]
        y = acc * _sigmoid(acc)
        if cc * cw < D_INNER:
            xs_ref[:, sl] = y
        else:
            bc_ref[:, cc * cw - D_INNER:(cc + 1) * cw - D_INNER] = y.astype(BF16)


def _conv(xbc, w, b, seq, ts=128):
    T = xbc.shape[0]
    hb = ts // HALO_ROWS
    last_halo = T // HALO_ROWS - 1
    row = lambda i: (i, 0)
    kern = functools.partial(_conv_kernel, steps_per_seq=seq // ts, ts=ts)
    return pl.pallas_call(
        kern,
        grid=(T // ts,),
        in_specs=[pl.BlockSpec((HALO_ROWS, CONV_DIM), lambda i: (jnp.maximum(i * hb - 1, 0), 0)),
                  pl.BlockSpec((ts, CONV_DIM), row),
                  pl.BlockSpec((HALO_ROWS, CONV_DIM), lambda i: (jnp.minimum((i + 1) * hb, last_halo), 0)),
                  _resident((CONV_TAPS, CONV_DIM)), _resident((1, CONV_DIM))],
        out_specs=[pl.BlockSpec((ts, D_INNER), row), pl.BlockSpec((ts, CONV_DIM - D_INNER), row)],
        out_shape=[jax.ShapeDtypeStruct((T, D_INNER), F32), jax.ShapeDtypeStruct((T, CONV_DIM - D_INNER), BF16)],
        compiler_params=_params(),
        name="conv_silu",
    )(xbc, xbc, xbc, w, b)


def _split3(v):
    hi = v.astype(BF16)
    r1 = v - hi.astype(F32)
    mid = r1.astype(BF16)
    lo = (r1 - mid.astype(F32)).astype(BF16)
    return jnp.concatenate([hi, mid, lo], axis=1)


def _expand_heads(v, e3_ref, first_col):
    lane = lax.broadcasted_iota(jnp.int32, v.shape, 1)
    keep = jnp.logical_and(lane >= first_col, lane < first_col + SSD_HEADS)
    return jnp.dot(_split3(jnp.where(keep, v, 0.0)), e3_ref[...], preferred_element_type=F32)


def _ssd_decays(dt_ref, dtb_ref, alog_ref):
    dt = _softplus(dt_ref[...] + dtb_ref[...])
    adt = dt * (-jnp.exp(alog_ref[...]))
    row = lax.broadcasted_iota(jnp.int32, (CHUNK, CHUNK), 0)
    col = lax.broadcasted_iota(jnp.int32, (CHUNK, CHUNK), 1)
    tril = (row >= col).astype(F32)
    triu = (row <= col).astype(F32)
    acs_f = jnp.dot(tril, adt, precision=lax.Precision.HIGHEST, preferred_element_type=F32)
    acs_b = jnp.dot(triu, adt, precision=lax.Precision.HIGHEST, preferred_element_type=F32)
    lane = lax.broadcasted_iota(jnp.int32, (CHUNK, DT_PAD), 1)
    acs = jnp.where(lane < SSD_HEADS, acs_f, acs_b)
    return dt, acs, row, col


_NT = (((1,), (1,)), ((), ()))
_TN = (((0,), (0,)), ((), ()))


def _ssd_fwd_kernel(xs_ref, bc_ref, dt_ref, dtb_ref, alog_ref, e3_ref, y_ref, s_ref, *, nc):
    c = pl.program_id(0)

    @pl.when(c % nc == 0)
    def _():
        s_ref[...] = jnp.zeros_like(s_ref)

    dt, acs, row, col = _ssd_decays(dt_ref, dtb_ref, alog_ref)
    acs_t = acs.T
    dt_t = dt.T
    decay_in = _expand_heads(jnp.exp(acs), e3_ref, 0)
    w_state = _expand_heads(jnp.exp(acs[CHUNK - 1:CHUNK, :] - acs) * dt, e3_ref, 0)
    xs = xs_ref[...]
    xw = (xs * w_state).astype(BF16)
    xb = xs.astype(BF16)
    lower = row >= col
    strict_lower = row > col
    strict_upper = row < col
    head_of_lane = lax.broadcasted_iota(jnp.int32, (CHUNK, GROUP_WIDTH), 1) // SSD_HEAD_DIM
    nb = SSD_GROUPS * SSD_STATE
    for g in range(SSD_GROUPS):
        gs = slice(g * GROUP_WIDTH, (g + 1) * GROUP_WIDTH)
        b_g = bc_ref[:, g * SSD_STATE:(g + 1) * SSD_STATE]
        c_g = bc_ref[:, nb + g * SSD_STATE:nb + (g + 1) * SSD_STATE]
        cb = lax.dot_general(c_g, b_g, _NT, preferred_element_type=F32)
        x_g = xb[:, gs]
        y_diag = None
        for r in range(HEADS_PER_GROUP):
            hf = g * HEADS_PER_GROUP + r
            hb = SSD_HEADS + hf
            diff = jnp.where(lower, acs[:, hf:hf + 1] - acs_t[hf:hf + 1, :],
                             acs[:, hb:hb + 1] - acs_t[hb:hb + 1, :])
            dt_f = dt_t[hf:hf + 1, :]
            dt_b = dt_t[hb:hb + 1, :]
            w = jnp.where(strict_lower, dt_f, jnp.where(strict_upper, dt_b, dt_f + dt_b))
            m = (cb * jnp.exp(diff) * w).astype(BF16)
            x_h = jnp.where(head_of_lane == r, x_g, jnp.zeros_like(x_g))
            t = jnp.dot(m, x_h, preferred_element_type=F32)
            y_diag = t if y_diag is None else y_diag + t
        s_g = s_ref[g]
        y_off = jnp.dot(c_g, s_g.astype(BF16), preferred_element_type=F32) * decay_in[:, gs]
        y_ref[:, gs] = y_diag + y_off
        upd = lax.dot_general(b_g, xw[:, gs], _TN, preferred_element_type=F32)
        s_ref[g] = s_g * decay_in[CHUNK - 1:CHUNK, gs] + upd


def _ssd_fwd(xs, bc, dt, dtb, alog, e3, seq):
    T = xs.shape[0]
    row = lambda c: (c, 0)
    kern = functools.partial(_ssd_fwd_kernel, nc=seq // CHUNK)
    return pl.pallas_call(
        kern,
        grid=(T // CHUNK,),
        in_specs=[pl.BlockSpec((CHUNK, D_INNER), row), pl.BlockSpec((CHUNK, CONV_DIM - D_INNER), row),
                  pl.BlockSpec((CHUNK, DT_PAD), row), _resident((1, DT_PAD)), _resident((1, DT_PAD)),
                  _resident(e3.shape)],
        out_specs=pl.BlockSpec((CHUNK, D_INNER), row),
        out_shape=jax.ShapeDtypeStruct((T, D_INNER), F32),
        scratch_shapes=[pltpu.VMEM((SSD_GROUPS, SSD_STATE, GROUP_WIDTH), F32)],
        compiler_params=_params(),
        name="ssd_fwd",
    )(xs, bc, dt, dtb, alog, e3)


def _ssd_bwd_kernel(xs_ref, bc_ref, dt_ref, dtb_ref, alog_ref, e3_ref, yp_ref, z_ref, gs_ref, ga_ref, x_ref,
                    dskip_ref, nw_ref, wso_ref, bg_ref, wo_ref, pmn_ref, out_ref, s_ref, *, nc):
    c = pl.program_id(0)

    @pl.when(c % nc == 0)
    def _():
        s_ref[...] = jnp.zeros_like(s_ref)

    dt, acs, _, _ = _ssd_decays(dt_ref, dtb_ref, alog_ref)
    decay_in = _expand_heads(jnp.exp(acs), e3_ref, SSD_HEADS)
    w_state = _expand_heads(jnp.exp(acs[0:1, :] - acs) * dt, e3_ref, SSD_HEADS)
    xs = xs_ref[...]
    xw = (xs * w_state).astype(BF16)
    nb = SSD_GROUPS * SSD_STATE
    normed = []
    for g in range(SSD_GROUPS):
        gs = slice(g * GROUP_WIDTH, (g + 1) * GROUP_WIDTH)
        b_g = bc_ref[:, g * SSD_STATE:(g + 1) * SSD_STATE]
        c_g = bc_ref[:, nb + g * SSD_STATE:nb + (g + 1) * SSD_STATE]
        s_g = s_ref[g]
        y_off = jnp.dot(c_g, s_g.astype(BF16), preferred_element_type=F32) * decay_in[:, gs]
        y = yp_ref[:, gs] + y_off + xs[:, gs] * dskip_ref[:, gs]
        z = z_ref[:, gs]
        u = y * (z * _sigmoid(z))
        normed.append((_rms(u, nw_ref[:, gs])).astype(BF16))
        upd = lax.dot_general(b_g, xw[:, gs], _TN, preferred_element_type=F32)
        s_ref[g] = s_g * decay_in[0:1, gs] + upd
    yn = jnp.concatenate(normed, axis=1)
    y_ssd = jnp.dot(yn, wso_ref[...], preferred_element_type=F32)
    gate = _sigmoid(gs_ref[...] + bg_ref[...])
    mix = (gate * y_ssd + ga_ref[...]).astype(BF16)
    mixed = jnp.dot(mix, wo_ref[...], preferred_element_type=F32)
    out_ref[...] = x_ref[...] + _rms(mixed, pmn_ref[...])


def _ssd_bwd(xs, bc, dt, dtb, alog, e3, yp, zg, ga, x, dskip, nw, wso, bg, wo, pmn, seq):
    T = xs.shape[0]
    last = T // CHUNK - 1
    rev = lambda c: (last - c, 0)
    kern = functools.partial(_ssd_bwd_kernel, nc=seq // CHUNK)
    return pl.pallas_call(
        kern,
        grid=(T // CHUNK,),
        in_specs=[pl.BlockSpec((CHUNK, D_INNER), rev), pl.BlockSpec((CHUNK, CONV_DIM - D_INNER), rev),
                  pl.BlockSpec((CHUNK, DT_PAD), rev), _resident((1, DT_PAD)), _resident((1, DT_PAD)),
                  _resident(e3.shape),
                  pl.BlockSpec((CHUNK, D_INNER), rev),
                  pl.BlockSpec((CHUNK, D_INNER), rev),
                  pl.BlockSpec((CHUNK, D_MODEL), lambda c: (last - c, 2)),
                  pl.BlockSpec((CHUNK, D_MODEL), rev), pl.BlockSpec((CHUNK, D_MODEL), rev),
                  _resident((1, D_INNER)), _resident((1, D_INNER)), _resident((D_INNER, D_MODEL)),
                  _resident((1, D_MODEL)), _resident((D_MODEL, D_MODEL)), _resident((1, D_MODEL))],
        out_specs=pl.BlockSpec((CHUNK, D_MODEL), rev),
        out_shape=jax.ShapeDtypeStruct((T, D_MODEL), F32),
        scratch_shapes=[pltpu.VMEM((SSD_GROUPS, SSD_STATE, GROUP_WIDTH), F32)],
        compiler_params=_params(),
        name="ssd_bwd_mix",
    )(xs, bc, dt, dtb, alog, e3, yp, zg, zg, ga, x, dskip, nw, wso, bg, wo, pmn)


def _attn_kernel(q_ref, kp_ref, kc_ref, kn_ref, vp_ref, vc_ref, vn_ref, bias_ref, sink_ref, gate_ref, bg_ref,
                 wao_ref, out_ref, *, nb):
    n = pl.program_id(0)
    first = (n % nb) == 0
    last = (n % nb) == nb - 1
    kcol = lax.broadcasted_iota(jnp.int32, (1, 3 * ATTN_BLOCK), 1)
    in_seq = jnp.logical_and(jnp.logical_or(kcol >= ATTN_BLOCK, jnp.logical_not(first)),
                             jnp.logical_or(kcol < 2 * ATTN_BLOCK, jnp.logical_not(last)))
    k = jnp.concatenate([kp_ref[...], kc_ref[...], kn_ref[...]], axis=0)
    v = jnp.concatenate([vp_ref[...], vc_ref[...], vn_ref[...]], axis=0)
    low_half = lax.broadcasted_iota(jnp.int32, (ATTN_BLOCK, LANES), 1) < HEAD_DIM
    outs = []
    for t in range(KV_HEADS // 2):
        kt = k[:, t * LANES:(t + 1) * LANES]
        vt = v[:, t * LANES:(t + 1) * LANES]
        qs = []
        for r in range(4):
            qt = q_ref[:, (4 * t + r) * LANES:(4 * t + r + 1) * LANES]
            zero = jnp.zeros_like(qt)
            qs += [jnp.where(low_half, qt, zero), jnp.where(low_half, zero, qt)]
        qz = jnp.concatenate(qs, axis=0)
        s = lax.dot_general(qz, kt, _NT, preferred_element_type=F32) + bias_ref[t]
        s = jnp.where(in_seq, s, -jnp.inf)
        sk = sink_ref[t]
        m = jnp.maximum(jnp.max(s, axis=-1, keepdims=True), sk)
        p = jnp.exp(s - m).astype(BF16)
        o = jnp.dot(p, jnp.concatenate([vt, jnp.ones_like(vt)], axis=1), preferred_element_type=F32)
        den = o[:, LANES:] + jnp.exp(sk - m)
        o = o[:, :LANES] * (1.0 / den)
        for r in range(4):
            lo = o[(2 * r) * ATTN_BLOCK:(2 * r + 1) * ATTN_BLOCK]
            hi = o[(2 * r + 1) * ATTN_BLOCK:(2 * r + 2) * ATTN_BLOCK]
            outs.append(jnp.where(low_half, lo, hi).astype(BF16))
    o = jnp.concatenate(outs, axis=1)
    y = jnp.dot(o, wao_ref[...], preferred_element_type=F32)
    out_ref[...] = _sigmoid(gate_ref[...] + bg_ref[...]) * y


def _attn(qkv, bias, sink, zg, bg, wao, seq):
    T = qkv.shape[0]
    nb = seq // ATTN_BLOCK
    nblk = T // ATTN_BLOCK
    kcol = ATTN_WIDTH // KV_WIDTH
    vcol = kcol + 1
    prev = lambda n: jnp.maximum(n - 1, 0)
    nxt = lambda n: jnp.minimum(n + 1, nblk - 1)
    kern = functools.partial(_attn_kernel, nb=nb)
    return pl.pallas_call(
        kern,
        grid=(nblk,),
        in_specs=[pl.BlockSpec((ATTN_BLOCK, ATTN_WIDTH), lambda n: (n, 0)),
                  pl.BlockSpec((ATTN_BLOCK, KV_WIDTH), lambda n: (prev(n), kcol)),
                  pl.BlockSpec((ATTN_BLOCK, KV_WIDTH), lambda n: (n, kcol)),
                  pl.BlockSpec((ATTN_BLOCK, KV_WIDTH), lambda n: (nxt(n), kcol)),
                  pl.BlockSpec((ATTN_BLOCK, KV_WIDTH), lambda n: (prev(n), vcol)),
                  pl.BlockSpec((ATTN_BLOCK, KV_WIDTH), lambda n: (n, vcol)),
                  pl.BlockSpec((ATTN_BLOCK, KV_WIDTH), lambda n: (nxt(n), vcol)),
                  _resident(bias.shape), _resident(sink.shape),
                  pl.BlockSpec((ATTN_BLOCK, D_MODEL), lambda n: (n, 3)),
                  _resident((1, D_MODEL)), _resident((ATTN_WIDTH, D_MODEL))],
        out_specs=pl.BlockSpec((ATTN_BLOCK, D_MODEL), lambda n: (n, 0)),
        out_shape=jax.ShapeDtypeStruct((T, D_MODEL), F32),
        compiler_params=_params(),
        name="attn",
    )(qkv, qkv, qkv, qkv, qkv, qkv, qkv, bias, sink, zg, bg, wao)


def _mlp_kernel(x_ref, g1_ref, w1_ref, w2_ref, g2_ref, out_ref):
    x = x_ref[...]
    h = _rms(x, g1_ref[...]).astype(BF16)
    step = 1024
    acc = None
    for j in range(D_FF // step):
        a = jnp.dot(h, w1_ref[:, j * step:(j + 1) * step], preferred_element_type=F32)
        a = jnp.square(jnp.maximum(a, 0.0)).astype(BF16)
        t = jnp.dot(a, w2_ref[j * step:(j + 1) * step, :], preferred_element_type=F32)
        acc = t if acc is None else acc + t
    out_ref[...] = x + _rms(acc, g2_ref[...])


def _mlp(x, g1, w1, w2, g2, tm=256):
    T = x.shape[0]
    row = lambda i: (i, 0)
    return pl.pallas_call(
        _mlp_kernel,
        grid=(T // tm,),
        in_specs=[pl.BlockSpec((tm, D_MODEL), row), _resident((1, D_MODEL)), _resident((D_MODEL, D_FF)),
                  _resident((D_FF, D_MODEL)), _resident((1, D_MODEL))],
        out_specs=pl.BlockSpec((tm, D_MODEL), row),
        out_shape=jax.ShapeDtypeStruct((T, D_MODEL), F32),
        compiler_params=_params(),
        name="mlp",
    )(x, g1, w1, w2, g2)


def _permute_heads(a, axis):
    shape = a.shape
    a = a.reshape(shape[:axis] + (KV_HEADS // 2, 2, 4) + shape[axis + 1:])
    a = jnp.swapaxes(a, axis + 1, axis + 2)
    return a.reshape(shape)


def _prep_w_in(w):
    dt0 = D_INNER + CONV_DIM
    q0 = dt0 + 2 * SSD_HEADS
    k0 = q0 + ATTN_WIDTH
    wzx = w[:, :dt0].astype(BF16)
    wq = w[:, q0:k0].reshape(D_MODEL, Q_HEADS, HEAD_DIM) * (HEAD_DIM ** -0.5)
    wq = _permute_heads(wq, 1).reshape(D_MODEL, ATTN_WIDTH).astype(BF16)
    wkvg = w[:, k0:].astype(BF16)
    wdt = jnp.pad(w[:, dt0:q0], ((0, 0), (0, DT_PAD - 2 * SSD_HEADS))).astype(BF16)
    return wzx, wq, wkvg, wdt


def _expand_matrix(first_col):
    e = np.zeros((3 * DT_PAD, D_INNER), np.float32)
    for p in range(3):
        for h in range(SSD_HEADS):
            e[p * DT_PAD + first_col + h, h * SSD_HEAD_DIM:(h + 1) * SSD_HEAD_DIM] = 1.0
    return jnp.asarray(e, BF16)


def _t5_bucket(rel):
    nb = N_BUCKETS // 2
    max_exact = nb // 2
    ret = jnp.where(rel > 0, nb, 0)
    n = jnp.abs(rel)
    nf = jnp.maximum(n, 1).astype(F32)
    large = max_exact + (jnp.log(nf / max_exact) / math.log(MAX_DISTANCE / max_exact)
                         * (nb - max_exact)).astype(jnp.int32)
    large = jnp.minimum(large, nb - 1)
    return ret + jnp.where(n < max_exact, n, large)


def _window_bias(rel_table):
    period = 4 * ATTN_BLOCK
    d = jnp.arange(period)
    d = jnp.where(d >= 3 * ATTN_BLOCK, d - period, d)
    rel = d - ATTN_BLOCK
    vec = rel_table[_t5_bucket(rel)].astype(F32)
    vec = jnp.where((jnp.abs(rel) <= WINDOW)[:, None], vec, -jnp.inf).T
    skew = jnp.tile(vec, (1, ATTN_BLOCK))[:, :ATTN_BLOCK * (period - 1)]
    bias = skew.reshape(Q_HEADS, ATTN_BLOCK, period - 1)[:, :, :3 * ATTN_BLOCK]
    return _permute_heads(bias, 0).reshape(KV_HEADS // 2, 8 * ATTN_BLOCK, 3 * ATTN_BLOCK)


def _stack_sink(sink):
    s = _permute_heads(sink.astype(F32), 0)
    return jnp.repeat(s, ATTN_BLOCK).reshape(KV_HEADS // 2, 8 * ATTN_BLOCK, 1)


def _pad_lanes(v):
    return jnp.pad(v.reshape(1, -1).astype(F32), ((0, 0), (0, DT_PAD - v.size)))


def kernel(x, pre_mix_norm, w_in, b_gate, conv_w, conv_b, dt_bias, a_log, d_skip, ssd_norm, w_ssd_out, attn_sink,
           rel_bias_table, w_attn_out, w_o, post_mix_norm, pre_mlp_norm, w_mlp_in, w_mlp_out, post_mlp_norm):
    bsz, seq, d = x.shape
    assert d == D_MODEL and seq % CHUNK == 0 and seq % ATTN_BLOCK == 0
    depth = w_in.shape[0]
    xf = x.reshape(bsz * seq, d)
    e3_fwd = _expand_matrix(0)
    e3_bwd = _expand_matrix(SSD_HEADS)
    bias = _window_bias(rel_bias_table)
    r1 = lambda v: v.reshape(1, -1)
    for l in range(depth):
        xbc, zg, qkv, dtr = _inproj(xf, r1(pre_mix_norm[l]), *_prep_w_in(w_in[l]))
        xs, bc = _conv(xbc, conv_w[l].reshape(CONV_TAPS, CONV_DIM), r1(conv_b[l]), seq)
        dtb = _pad_lanes(dt_bias[l])
        alog = _pad_lanes(a_log[l])
        yp = _ssd_fwd(xs, bc, dtr, dtb, alog, e3_fwd, seq)
        wao = _permute_heads(w_attn_out[l].reshape(Q_HEADS, HEAD_DIM, D_MODEL), 0).reshape(ATTN_WIDTH, D_MODEL)
        ga = _attn(qkv, bias, _stack_sink(attn_sink[l]), zg, r1(b_gate[l, D_MODEL:]), wao.astype(BF16), seq)
        dskip = r1(jnp.repeat(d_skip[l], SSD_HEAD_DIM))
        xf = _ssd_bwd(xs, bc, dtr, dtb, alog, e3_bwd, yp, zg, ga, xf, dskip, r1(ssd_norm[l]),
                      w_ssd_out[l].astype(BF16), r1(b_gate[l, :D_MODEL]), w_o[l].astype(BF16),
                      r1(post_mix_norm[l]), seq)
        xf = _mlp(xf, r1(pre_mlp_norm[l]), w_mlp_in[l].astype(BF16), w_mlp_out[l].astype(BF16),
                  r1(post_mlp_norm[l]))
    return xf.reshape(bsz, seq, d)
```

```python
import functools
import math

import numpy as np
import jax
import jax.numpy as jnp
from jax import lax
from jax.experimental import pallas as pl
from jax.experimental.pallas import tpu as pltpu

F32 = jnp.float32
BF16 = jnp.bfloat16
EPS = 1e-6

D_MODEL = 1024
D_INNER = 2048
SSD_HEADS = 32
SSD_HEAD_DIM = 64
SSD_GROUPS = 8
HEADS_PER_GROUP = SSD_HEADS // SSD_GROUPS
GROUP_WIDTH = D_INNER // SSD_GROUPS
SSD_STATE = 128
CHUNK = 128
CONV_TAPS = 5
CONV_DIM = D_INNER + 2 * SSD_GROUPS * SSD_STATE
Q_HEADS = 16
KV_HEADS = 4
HEAD_DIM = 64
ATTN_WIDTH = Q_HEADS * HEAD_DIM
KV_WIDTH = KV_HEADS * HEAD_DIM
WINDOW = 128
ATTN_BLOCK = 128
N_BUCKETS = 32
MAX_DISTANCE = 128
D_FF = 4 * D_MODEL
DT_PAD = 128
LANES = 128
HALO_ROWS = 8

ZG_WIDTH = D_INNER + 2 * D_MODEL
QKV_WIDTH = ATTN_WIDTH + 2 * KV_WIDTH

VMEM_LIMIT = 56 * 1024 * 1024


def _params(n_axes=1):
    return pltpu.CompilerParams(dimension_semantics=("arbitrary",) * n_axes,
                                vmem_limit_bytes=VMEM_LIMIT)


def _resident(shape):
    nd = len(shape)
    return pl.BlockSpec(shape, lambda *_: (0,) * nd, pipeline_mode=pl.Buffered(1))


def _rms(x, g):
    ms = jnp.mean(x * x, axis=-1, keepdims=True)
    return x * lax.rsqrt(ms + EPS) * g


def _sigmoid(x):
    return 1.0 / (1.0 + jnp.exp(-x))


def _softplus(x):
    return jnp.maximum(x, 0.0) + jnp.log1p(jnp.exp(-jnp.abs(x)))


def _inproj_kernel(xp_ref, x_ref, xn_ref, g_ref, wzx_ref, wq_ref, wkvg_ref, wdt_ref, cw_ref, cb_ref,
                   xs_ref, bc_ref, zg_ref, qkv_ref, dt_ref, *, steps_per_seq, tm):
    i = pl.program_id(0)
    first = (i % steps_per_seq) == 0
    last = (i % steps_per_seq) == steps_per_seq - 1
    g = g_ref[...]
    x = x_ref[...]
    h = _rms(x, g).astype(BF16)
    h_ext = _rms(jnp.concatenate([xp_ref[...], x, xn_ref[...]], axis=0), g).astype(BF16)

    def mm(w_ref, lo, hi):
        return jnp.dot(h, w_ref[:, lo:hi], preferred_element_type=F32)

    cw = 256
    sub = lax.broadcasted_iota(jnp.int32, (tm, cw), 0) % HALO_ROWS

    def shifted(ext, d):
        cur = ext[HALO_ROWS:HALO_ROWS + tm]
        if d == 0:
            return cur
        if d < 0:
            blend = jnp.where(sub < HALO_ROWS + d, cur, ext[:tm])
            rot = -d
        else:
            blend = jnp.where(sub >= d, cur, ext[2 * HALO_ROWS:])
            rot = HALO_ROWS - d
        blend = blend.reshape(tm // HALO_ROWS, HALO_ROWS, cw)
        return pltpu.roll(blend, rot, axis=1).reshape(tm, cw)

    def project_xbc(cc):
        ext = jnp.dot(h_ext, wzx_ref[:, D_INNER + cc * cw:D_INNER + (cc + 1) * cw], preferred_element_type=F32)
        return jnp.concatenate([jnp.where(first, 0.0, ext[:HALO_ROWS]), ext[HALO_ROWS:HALO_ROWS + tm],
                                jnp.where(last, 0.0, ext[HALO_ROWS + tm:])], axis=0)

    def conv_silu(cc, ext):
        sl = slice(cc * cw, (cc + 1) * cw)
        acc = None
        for k in range(CONV_TAPS):
            term = shifted(ext, k - CONV_TAPS // 2) * cw_ref[k:k + 1, sl]
            acc = term if acc is None else acc + term
        acc = acc + cb_ref[:, sl]
        y = acc * _sigmoid(acc)
        if cc * cw < D_INNER:
            xs_ref[:, sl] = y
        else:
            bc_ref[:, cc * cw - D_INNER:(cc + 1) * cw - D_INNER] = y.astype(BF16)

    step = 1024

    def z_part(j):
        zg_ref[:, j * step:(j + 1) * step] = mm(wzx_ref, j * step, (j + 1) * step)

    def gate_part(j):
        zg_ref[:, D_INNER + j * step:D_INNER + (j + 1) * step] = mm(
            wkvg_ref, 2 * KV_WIDTH + j * step, 2 * KV_WIDTH + (j + 1) * step)

    def q_part():
        qkv_ref[:, :ATTN_WIDTH] = mm(wq_ref, 0, ATTN_WIDTH).astype(BF16)

    def kv_dt_part():
        qkv_ref[:, ATTN_WIDTH:] = mm(wkvg_ref, 0, 2 * KV_WIDTH).astype(BF16)
        dt_ref[...] = mm(wdt_ref, 0, DT_PAD)

    plain = [functools.partial(z_part, 0), functools.partial(z_part, 1), functools.partial(gate_part, 0),
             functools.partial(gate_part, 1), q_part, kv_dt_part]
    n_chunks = CONV_DIM // cw
    ext = project_xbc(0)
    for cc in range(n_chunks):
        if cc < len(plain):
            plain[cc]()
        nxt = project_xbc(cc + 1) if cc + 1 < n_chunks else None
        conv_silu(cc, ext)
        ext = nxt


def _inproj(x, g, wzx, wq, wkvg, wdt, conv_w, conv_b, seq, tm=256):
    T = x.shape[0]
    assert seq % tm == 0
    hb = tm // HALO_ROWS
    last_halo = T // HALO_ROWS - 1
    row = lambda i: (i, 0)
    kern = functools.partial(_inproj_kernel, steps_per_seq=seq // tm, tm=tm)
    return pl.pallas_call(
        kern,
        grid=(T // tm,),
        in_specs=[pl.BlockSpec((HALO_ROWS, D_MODEL), lambda i: (jnp.maximum(i * hb - 1, 0), 0)),
                  pl.BlockSpec((tm, D_MODEL), row),
                  pl.BlockSpec((HALO_ROWS, D_MODEL), lambda i: (jnp.minimum((i + 1) * hb, last_halo), 0)),
                  _resident((1, D_MODEL)), _resident(wzx.shape), _resident(wq.shape), _resident(wkvg.shape),
                  _resident(wdt.shape), _resident((CONV_TAPS, CONV_DIM)), _resident((1, CONV_DIM))],
        out_specs=[pl.BlockSpec((tm, D_INNER), row), pl.BlockSpec((tm, CONV_DIM - D_INNER), row),
                   pl.BlockSpec((tm, ZG_WIDTH), row), pl.BlockSpec((tm, QKV_WIDTH), row),
                   pl.BlockSpec((tm, DT_PAD), row)],
        out_shape=[jax.ShapeDtypeStruct((T, D_INNER), F32), jax.ShapeDtypeStruct((T, CONV_DIM - D_INNER), BF16),
                   jax.ShapeDtypeStruct((T, ZG_WIDTH), F32), jax.ShapeDtypeStruct((T, QKV_WIDTH), BF16),
                   jax.ShapeDtypeStruct((T, DT_PAD), F32)],
        compiler_params=_params(),
        name="inproj_conv",
    )(x, x, x, g, wzx, wq, wkvg, wdt, conv_w, conv_b)


def _split3(v):
    hi = v.astype(BF16)
    r1 = v - hi.astype(F32)
    mid = r1.astype(BF16)
    lo = (r1 - mid.astype(F32)).astype(BF16)
    return jnp.concatenate([hi, mid, lo], axis=1)


def _expand_heads(v, e3_ref, first_col):
    lane = lax.broadcasted_iota(jnp.int32, v.shape, 1)
    keep = jnp.logical_and(lane >= first_col, lane < first_col + SSD_HEADS)
    return jnp.dot(_split3(jnp.where(keep, v, 0.0)), e3_ref[...], preferred_element_type=F32)


def _ssd_decays(dt_ref, dtb_ref, alog_ref):
    dt = _softplus(dt_ref[...] + dtb_ref[...])
    adt = dt * (-jnp.exp(alog_ref[...]))
    row = lax.broadcasted_iota(jnp.int32, (CHUNK, CHUNK), 0)
    col = lax.broadcasted_iota(jnp.int32, (CHUNK, CHUNK), 1)
    tril = (row >= col).astype(F32)
    triu = (row <= col).astype(F32)
    acs_f = jnp.dot(tril, adt, precision=lax.Precision.HIGHEST, preferred_element_type=F32)
    acs_b = jnp.dot(triu, adt, precision=lax.Precision.HIGHEST, preferred_element_type=F32)
    lane = lax.broadcasted_iota(jnp.int32, (CHUNK, DT_PAD), 1)
    acs = jnp.where(lane < SSD_HEADS, acs_f, acs_b)
    return dt, acs, row, col


_NT = (((1,), (1,)), ((), ()))
_TN = (((0,), (0,)), ((), ()))


def _ssd_fwd_kernel(xs_ref, bc_ref, dt_ref, dtb_ref, alog_ref, e3_ref, y_ref, s_ref, *, nc):
    c = pl.program_id(0)

    @pl.when(c % nc == 0)
    def _():
        s_ref[...] = jnp.zeros_like(s_ref)

    dt, acs, row, col = _ssd_decays(dt_ref, dtb_ref, alog_ref)
    acs_t = acs.T
    dt_t = dt.T
    decay_in = _expand_heads(jnp.exp(acs), e3_ref, 0)
    w_state = _expand_heads(jnp.exp(acs[CHUNK - 1:CHUNK, :] - acs) * dt, e3_ref, 0)
    xs = xs_ref[...]
    xw = (xs * w_state).astype(BF16)
    xb = xs.astype(BF16)
    lower = row >= col
    strict_lower = row > col
    strict_upper = row < col
    head_of_lane = lax.broadcasted_iota(jnp.int32, (CHUNK, GROUP_WIDTH), 1) // SSD_HEAD_DIM
    nb = SSD_GROUPS * SSD_STATE
    for g in range(SSD_GROUPS):
        gs = slice(g * GROUP_WIDTH, (g + 1) * GROUP_WIDTH)
        b_g = bc_ref[:, g * SSD_STATE:(g + 1) * SSD_STATE]
        c_g = bc_ref[:, nb + g * SSD_STATE:nb + (g + 1) * SSD_STATE]
        cb = lax.dot_general(c_g, b_g, _NT, preferred_element_type=F32)
        x_g = xb[:, gs]
        y_diag = None
        for r in range(HEADS_PER_GROUP):
            hf = g * HEADS_PER_GROUP + r
            hb = SSD_HEADS + hf
            diff = jnp.where(lower, acs[:, hf:hf + 1] - acs_t[hf:hf + 1, :],
                             acs[:, hb:hb + 1] - acs_t[hb:hb + 1, :])
            dt_f = dt_t[hf:hf + 1, :]
            dt_b = dt_t[hb:hb + 1, :]
            w = jnp.where(strict_lower, dt_f, jnp.where(strict_upper, dt_b, dt_f + dt_b))
            m = (cb * jnp.exp(diff) * w).astype(BF16)
            x_h = jnp.where(head_of_lane == r, x_g, jnp.zeros_like(x_g))
            t = jnp.dot(m, x_h, preferred_element_type=F32)
            y_diag = t if y_diag is None else y_diag + t
        s_g = s_ref[g]
        y_off = jnp.dot(c_g, s_g.astype(BF16), preferred_element_type=F32) * decay_in[:, gs]
        y_ref[:, gs] = y_diag + y_off
        upd = lax.dot_general(b_g, xw[:, gs], _TN, preferred_element_type=F32)
        s_ref[g] = s_g * decay_in[CHUNK - 1:CHUNK, gs] + upd


def _ssd_fwd(xs, bc, dt, dtb, alog, e3, seq):
    T = xs.shape[0]
    row = lambda c: (c, 0)
    kern = functools.partial(_ssd_fwd_kernel, nc=seq // CHUNK)
    return pl.pallas_call(
        kern,
        grid=(T // CHUNK,),
        in_specs=[pl.BlockSpec((CHUNK, D_INNER), row), pl.BlockSpec((CHUNK, CONV_DIM - D_INNER), row),
                  pl.BlockSpec((CHUNK, DT_PAD), row), _resident((1, DT_PAD)), _resident((1, DT_PAD)),
                  _resident(e3.shape)],
        out_specs=pl.BlockSpec((CHUNK, D_INNER), row),
        out_shape=jax.ShapeDtypeStruct((T, D_INNER), F32),
        scratch_shapes=[pltpu.VMEM((SSD_GROUPS, SSD_STATE, GROUP_WIDTH), F32)],
        compiler_params=_params(),
        name="ssd_fwd",
    )(xs, bc, dt, dtb, alog, e3)


def _ssd_bwd_kernel(xs_ref, bc_ref, dt_ref, dtb_ref, alog_ref, e3_ref, yp_ref, z_ref, gs_ref, ga_ref, x_ref,
                    dskip_ref, nw_ref, wso_ref, bg_ref, wo_ref, pmn_ref, out_ref, s_ref, *, nc):
    c = pl.program_id(0)

    @pl.when(c % nc == 0)
    def _():
        s_ref[...] = jnp.zeros_like(s_ref)

    dt, acs, _, _ = _ssd_decays(dt_ref, dtb_ref, alog_ref)
    decay_in = _expand_heads(jnp.exp(acs), e3_ref, SSD_HEADS)
    w_state = _expand_heads(jnp.exp(acs[0:1, :] - acs) * dt, e3_ref, SSD_HEADS)
    xs = xs_ref[...]
    xw = (xs * w_state).astype(BF16)
    nb = SSD_GROUPS * SSD_STATE
    normed = []
    for g in range(SSD_GROUPS):
        gs = slice(g * GROUP_WIDTH, (g + 1) * GROUP_WIDTH)
        b_g = bc_ref[:, g * SSD_STATE:(g + 1) * SSD_STATE]
        c_g = bc_ref[:, nb + g * SSD_STATE:nb + (g + 1) * SSD_STATE]
        s_g = s_ref[g]
        y_off = jnp.dot(c_g, s_g.astype(BF16), preferred_element_type=F32) * decay_in[:, gs]
        y = yp_ref[:, gs] + y_off + xs[:, gs] * dskip_ref[:, gs]
        z = z_ref[:, gs]
        u = y * (z * _sigmoid(z))
        normed.append((_rms(u, nw_ref[:, gs])).astype(BF16))
        upd = lax.dot_general(b_g, xw[:, gs], _TN, preferred_element_type=F32)
        s_ref[g] = s_g * decay_in[0:1, gs] + upd
    yn = jnp.concatenate(normed, axis=1)
    y_ssd = jnp.dot(yn, wso_ref[...], preferred_element_type=F32)
    gate = _sigmoid(gs_ref[...] + bg_ref[...])
    mix = (gate * y_ssd + ga_ref[...]).astype(BF16)
    mixed = jnp.dot(mix, wo_ref[...], preferred_element_type=F32)
    out_ref[...] = x_ref[...] + _rms(mixed, pmn_ref[...])


def _ssd_bwd(xs, bc, dt, dtb, alog, e3, yp, zg, ga, x, dskip, nw, wso, bg, wo, pmn, seq):
    T = xs.shape[0]
    last = T // CHUNK - 1
    rev = lambda c: (last - c, 0)
    kern = functools.partial(_ssd_bwd_kernel, nc=seq // CHUNK)
    return pl.pallas_call(
        kern,
        grid=(T // CHUNK,),
        in_specs=[pl.BlockSpec((CHUNK, D_INNER), rev), pl.BlockSpec((CHUNK, CONV_DIM - D_INNER), rev),
                  pl.BlockSpec((CHUNK, DT_PAD), rev), _resident((1, DT_PAD)), _resident((1, DT_PAD)),
                  _resident(e3.shape),
                  pl.BlockSpec((CHUNK, D_INNER), rev),
                  pl.BlockSpec((CHUNK, D_INNER), rev),
                  pl.BlockSpec((CHUNK, D_MODEL), lambda c: (last - c, 2)),
                  pl.BlockSpec((CHUNK, D_MODEL), rev), pl.BlockSpec((CHUNK, D_MODEL), rev),
                  _resident((1, D_INNER)), _resident((1, D_INNER)), _resident((D_INNER, D_MODEL)),
                  _resident((1, D_MODEL)), _resident((D_MODEL, D_MODEL)), _resident((1, D_MODEL))],
        out_specs=pl.BlockSpec((CHUNK, D_MODEL), rev),
        out_shape=jax.ShapeDtypeStruct((T, D_MODEL), F32),
        scratch_shapes=[pltpu.VMEM((SSD_GROUPS, SSD_STATE, GROUP_WIDTH), F32)],
        compiler_params=_params(),
        name="ssd_bwd_mix",
    )(xs, bc, dt, dtb, alog, e3, yp, zg, zg, ga, x, dskip, nw, wso, bg, wo, pmn)


def _attn_kernel(q_ref, kp_ref, kc_ref, kn_ref, vp_ref, vc_ref, vn_ref, bias_ref, sink_ref, gate_ref, bg_ref,
                 wao_ref, out_ref, *, nb):
    n = pl.program_id(0)
    first = (n % nb) == 0
    last = (n % nb) == nb - 1
    kcol = lax.broadcasted_iota(jnp.int32, (1, 3 * ATTN_BLOCK), 1)
    in_seq = jnp.logical_and(jnp.logical_or(kcol >= ATTN_BLOCK, jnp.logical_not(first)),
                             jnp.logical_or(kcol < 2 * ATTN_BLOCK, jnp.logical_not(last)))
    k = jnp.concatenate([kp_ref[...], kc_ref[...], kn_ref[...]], axis=0)
    v = jnp.concatenate([vp_ref[...], vc_ref[...], vn_ref[...]], axis=0)
    low_half = lax.broadcasted_iota(jnp.int32, (ATTN_BLOCK, LANES), 1) < HEAD_DIM
    outs = []
    for t in range(KV_HEADS // 2):
        kt = k[:, t * LANES:(t + 1) * LANES]
        vt = v[:, t * LANES:(t + 1) * LANES]
        qs = []
        for r in range(4):
            qt = q_ref[:, (4 * t + r) * LANES:(4 * t + r + 1) * LANES]
            zero = jnp.zeros_like(qt)
            qs += [jnp.where(low_half, qt, zero), jnp.where(low_half, zero, qt)]
        qz = jnp.concatenate(qs, axis=0)
        s = lax.dot_general(qz, kt, _NT, preferred_element_type=F32) + bias_ref[t]
        s = jnp.where(in_seq, s, -jnp.inf)
        sk = sink_ref[t]
        m = jnp.maximum(jnp.max(s, axis=-1, keepdims=True), sk)
        p = jnp.exp(s - m).astype(BF16)
        o = jnp.dot(p, jnp.concatenate([vt, jnp.ones_like(vt)], axis=1), preferred_element_type=F32)
        den = o[:, LANES:] + jnp.exp(sk - m)
        o = o[:, :LANES] * (1.0 / den)
        for r in range(4):
            lo = o[(2 * r) * ATTN_BLOCK:(2 * r + 1) * ATTN_BLOCK]
            hi = o[(2 * r + 1) * ATTN_BLOCK:(2 * r + 2) * ATTN_BLOCK]
            outs.append(jnp.where(low_half, lo, hi).astype(BF16))
    o = jnp.concatenate(outs, axis=1)
    y = jnp.dot(o, wao_ref[...], preferred_element_type=F32)
    out_ref[...] = _sigmoid(gate_ref[...] + bg_ref[...]) * y


def _attn(qkv, bias, sink, zg, bg, wao, seq):
    T = qkv.shape[0]
    nb = seq // ATTN_BLOCK
    nblk = T // ATTN_BLOCK
    kcol = ATTN_WIDTH // KV_WIDTH
    vcol = kcol + 1
    prev = lambda n: jnp.maximum(n - 1, 0)
    nxt = lambda n: jnp.minimum(n + 1, nblk - 1)
    kern = functools.partial(_attn_kernel, nb=nb)
    return pl.pallas_call(
        kern,
        grid=(nblk,),
        in_specs=[pl.BlockSpec((ATTN_BLOCK, ATTN_WIDTH), lambda n: (n, 0)),
                  pl.BlockSpec((ATTN_BLOCK, KV_WIDTH), lambda n: (prev(n), kcol)),
                  pl.BlockSpec((ATTN_BLOCK, KV_WIDTH), lambda n: (n, kcol)),
                  pl.BlockSpec((ATTN_BLOCK, KV_WIDTH), lambda n: (nxt(n), kcol)),
                  pl.BlockSpec((ATTN_BLOCK, KV_WIDTH), lambda n: (prev(n), vcol)),
                  pl.BlockSpec((ATTN_BLOCK, KV_WIDTH), lambda n: (n, vcol)),
                  pl.BlockSpec((ATTN_BLOCK, KV_WIDTH), lambda n: (nxt(n), vcol)),
                  _resident(bias.shape), _resident(sink.shape),
                  pl.BlockSpec((ATTN_BLOCK, D_MODEL), lambda n: (n, 3)),
                  _resident((1, D_MODEL)), _resident((ATTN_WIDTH, D_MODEL))],
        out_specs=pl.BlockSpec((ATTN_BLOCK, D_MODEL), lambda n: (n, 0)),
        out_shape=jax.ShapeDtypeStruct((T, D_MODEL), F32),
        compiler_params=_params(),
        name="attn",
    )(qkv, qkv, qkv, qkv, qkv, qkv, qkv, bias, sink, zg, bg, wao)


def _mlp_kernel(x_ref, g1_ref, w1_ref, w2_ref, g2_ref, out_ref):
    x = x_ref[...]
    h = _rms(x, g1_ref[...]).astype(BF16)
    step = 1024
    acc = None
    for j in range(D_FF // step):
        a = jnp.dot(h, w1_ref[:, j * step:(j + 1) * step], preferred_element_type=F32)
        a = jnp.square(jnp.maximum(a, 0.0)).astype(BF16)
        t = jnp.dot(a, w2_ref[j * step:(j + 1) * step, :], preferred_element_type=F32)
        acc = t if acc is None else acc + t
    out_ref[...] = x + _rms(acc, g2_ref[...])


def _mlp(x, g1, w1, w2, g2, tm=256):
    T = x.shape[0]
    row = lambda i: (i, 0)
    return pl.pallas_call(
        _mlp_kernel,
        grid=(T // tm,),
        in_specs=[pl.BlockSpec((tm, D_MODEL), row), _resident((1, D_MODEL)), _resident((D_MODEL, D_FF)),
                  _resident((D_FF, D_MODEL)), _resident((1, D_MODEL))],
        out_specs=pl.BlockSpec((tm, D_MODEL), row),
        out_shape=jax.ShapeDtypeStruct((T, D_MODEL), F32),
        compiler_params=_params(),
        name="mlp",
    )(x, g1, w1, w2, g2)


def _permute_heads(a, axis):
    shape = a.shape
    a = a.reshape(shape[:axis] + (KV_HEADS // 2, 2, 4) + shape[axis + 1:])
    a = jnp.swapaxes(a, axis + 1, axis + 2)
    return a.reshape(shape)


def _prep_w_in(w):
    dt0 = D_INNER + CONV_DIM
    q0 = dt0 + 2 * SSD_HEADS
    k0 = q0 + ATTN_WIDTH
    wzx = w[:, :dt0].astype(BF16)
    wq = w[:, q0:k0].reshape(D_MODEL, Q_HEADS, HEAD_DIM) * (HEAD_DIM ** -0.5)
    wq = _permute_heads(wq, 1).reshape(D_MODEL, ATTN_WIDTH).astype(BF16)
    wkvg = w[:, k0:].astype(BF16)
    wdt = jnp.pad(w[:, dt0:q0], ((0, 0), (0, DT_PAD - 2 * SSD_HEADS))).astype(BF16)
    return wzx, wq, wkvg, wdt


def _expand_matrix(first_col):
    e = np.zeros((3 * DT_PAD, D_INNER), np.float32)
    for p in range(3):
        for h in range(SSD_HEADS):
            e[p * DT_PAD + first_col + h, h * SSD_HEAD_DIM:(h + 1) * SSD_HEAD_DIM] = 1.0
    return jnp.asarray(e, BF16)


def _t5_bucket(rel):
    nb = N_BUCKETS // 2
    max_exact = nb // 2
    ret = jnp.where(rel > 0, nb, 0)
    n = jnp.abs(rel)
    nf = jnp.maximum(n, 1).astype(F32)
    large = max_exact + (jnp.log(nf / max_exact) / math.log(MAX_DISTANCE / max_exact)
                         * (nb - max_exact)).astype(jnp.int32)
    large = jnp.minimum(large, nb - 1)
    return ret + jnp.where(n < max_exact, n, large)


def _window_bias(rel_table):
    period = 4 * ATTN_BLOCK
    d = jnp.arange(period)
    d = jnp.where(d >= 3 * ATTN_BLOCK, d - period, d)
    rel = d - ATTN_BLOCK
    vec = rel_table[_t5_bucket(rel)].astype(F32)
    vec = jnp.where((jnp.abs(rel) <= WINDOW)[:, None], vec, -jnp.inf).T
    skew = jnp.tile(vec, (1, ATTN_BLOCK))[:, :ATTN_BLOCK * (period - 1)]
    bias = skew.reshape(Q_HEADS, ATTN_BLOCK, period - 1)[:, :, :3 * ATTN_BLOCK]
    return _permute_heads(bias, 0).reshape(KV_HEADS // 2, 8 * ATTN_BLOCK, 3 * ATTN_BLOCK)


def _stack_sink(sink):
    s = _permute_heads(sink.astype(F32), 0)
    return jnp.repeat(s, ATTN_BLOCK).reshape(KV_HEADS // 2, 8 * ATTN_BLOCK, 1)


def _pad_lanes(v):
    return jnp.pad(v.reshape(1, -1).astype(F32), ((0, 0), (0, DT_PAD - v.size)))


def kernel(x, pre_mix_norm, w_in, b_gate, conv_w, conv_b, dt_bias, a_log, d_skip, ssd_norm, w_ssd_out, attn_sink,
           rel_bias_table, w_attn_out, w_o, post_mix_norm, pre_mlp_norm, w_mlp_in, w_mlp_out, post_mlp_norm):
    bsz, seq, d = x.shape
    assert d == D_MODEL and seq % CHUNK == 0 and seq % ATTN_BLOCK == 0
    depth = w_in.shape[0]
    xf = x.reshape(bsz * seq, d)
    e3_fwd = _expand_matrix(0)
    e3_bwd = _expand_matrix(SSD_HEADS)
    bias = _window_bias(rel_bias_table)
    r1 = lambda v: v.reshape(1, -1)
    for l in range(depth):
        xs, bc, zg, qkv, dtr = _inproj(xf, r1(pre_mix_norm[l]), *_prep_w_in(w_in[l]),
                                       conv_w[l].reshape(CONV_TAPS, CONV_DIM), r1(conv_b[l]), seq)
        dtb = _pad_lanes(dt_bias[l])
        alog = _pad_lanes(a_log[l])
        yp = _ssd_fwd(xs, bc, dtr, dtb, alog, e3_fwd, seq)
        wao = _permute_heads(w_attn_out[l].reshape(Q_HEADS, HEAD_DIM, D_MODEL), 0).reshape(ATTN_WIDTH, D_MODEL)
        ga = _attn(qkv, bias, _stack_sink(attn_sink[l]), zg, r1(b_gate[l, D_MODEL:]), wao.astype(BF16), seq)
        dskip = r1(jnp.repeat(d_skip[l], SSD_HEAD_DIM))
        xf = _ssd_bwd(xs, bc, dtr, dtb, alog, e3_bwd, yp, zg, ga, xf, dskip, r1(ssd_norm[l]),
                      w_ssd_out[l].astype(BF16), r1(b_gate[l, :D_MODEL]), w_o[l].astype(BF16),
                      r1(post_mix_norm[l]), seq)
        xf = _mlp(xf, r1(pre_mlp_norm[l]), w_mlp_in[l].astype(BF16), w_mlp_out[l].astype(BF16),
                  r1(post_mlp_norm[l]))
    return xf.reshape(bsz, seq, d)
```

```python
import functools
import math

import numpy as np
import jax
import jax.numpy as jnp
from jax import lax
from jax.experimental import pallas as pl
from jax.experimental.pallas import tpu as pltpu

F32 = jnp.float32
BF16 = jnp.bfloat16
EPS = 1e-6

D_MODEL = 1024
D_INNER = 2048
SSD_HEADS = 32
SSD_HEAD_DIM = 64
SSD_GROUPS = 8
HEADS_PER_GROUP = SSD_HEADS // SSD_GROUPS
GROUP_WIDTH = D_INNER // SSD_GROUPS
SSD_STATE = 128
CHUNK = 128
CONV_TAPS = 5
CONV_DIM = D_INNER + 2 * SSD_GROUPS * SSD_STATE
Q_HEADS = 16
KV_HEADS = 4
HEAD_DIM = 64
ATTN_WIDTH = Q_HEADS * HEAD_DIM
KV_WIDTH = KV_HEADS * HEAD_DIM
WINDOW = 128
ATTN_BLOCK = 128
N_BUCKETS = 32
MAX_DISTANCE = 128
D_FF = 4 * D_MODEL
LANES = 128
HALO_ROWS = 8
DT_WIDTH = 2 * LANES
BWD_LANE = 3 * SSD_HEADS

ZG_WIDTH = D_INNER + 2 * D_MODEL
QKV_WIDTH = ATTN_WIDTH + 2 * KV_WIDTH
SSD_STEP_CHUNKS = 2

VMEM_LIMIT = 56 * 1024 * 1024


def _params(n_axes=1, flags=None):
    return pltpu.CompilerParams(dimension_semantics=("arbitrary",) * n_axes,
                                vmem_limit_bytes=VMEM_LIMIT, flags=flags)


def _resident(shape):
    nd = len(shape)
    return pl.BlockSpec(shape, lambda *_: (0,) * nd, pipeline_mode=pl.Buffered(1))


def _rms(x, g):
    ms = jnp.mean(x * x, axis=-1, keepdims=True)
    return x * lax.rsqrt(ms + EPS) * g


def _sigmoid(x):
    return 1.0 / (1.0 + jnp.exp(-x))


def _softplus(x):
    return jnp.maximum(x, 0.0) + jnp.log1p(jnp.exp(-jnp.abs(x)))


def _inproj_kernel(xp_ref, x_ref, xn_ref, g_ref, wzx_ref, wq_ref, wkvg_ref, wdt_ref, cw_ref, cb_ref,
                   xs_ref, bc_ref, zg_ref, qkv_ref, dt_ref, ext_ref, *, steps_per_seq, tm, cw):
    i = pl.program_id(0)
    first = (i % steps_per_seq) == 0
    last = (i % steps_per_seq) == steps_per_seq - 1
    g = g_ref[...]
    x = x_ref[...]
    h = _rms(x, g).astype(BF16)
    h_ext = _rms(jnp.concatenate([xp_ref[...], x, xn_ref[...]], axis=0), g).astype(BF16)

    def mm(w_ref, lo, hi):
        return jnp.dot(h, w_ref[:, lo:hi], preferred_element_type=F32)

    sub = lax.broadcasted_iota(jnp.int32, (tm, cw), 0) % HALO_ROWS

    def shifted(slot, d):
        cur = ext_ref[slot, HALO_ROWS:HALO_ROWS + tm, :]
        if d == 0:
            return cur
        if d < 0:
            blend = jnp.where(sub < HALO_ROWS + d, cur, ext_ref[slot, 0:tm, :])
            rot = -d
        else:
            blend = jnp.where(sub >= d, cur, ext_ref[slot, 2 * HALO_ROWS:2 * HALO_ROWS + tm, :])
            rot = HALO_ROWS - d
        blend = blend.reshape(tm // HALO_ROWS, HALO_ROWS, cw)
        return pltpu.roll(blend, rot, axis=1).reshape(tm, cw)

    def project_xbc(cc):
        slot = (cc + i) % 2
        ext = jnp.dot(h_ext, wzx_ref[:, D_INNER + cc * cw:D_INNER + (cc + 1) * cw], preferred_element_type=F32)
        ext_ref[slot, 0:HALO_ROWS, :] = jnp.where(first, 0.0, ext[:HALO_ROWS])
        ext_ref[slot, HALO_ROWS:HALO_ROWS + tm, :] = ext[HALO_ROWS:HALO_ROWS + tm]
        ext_ref[slot, HALO_ROWS + tm:, :] = jnp.where(last, 0.0, ext[HALO_ROWS + tm:])

    def conv_silu(cc):
        sl = slice(cc * cw, (cc + 1) * cw)
        acc = None
        for k in range(CONV_TAPS):
            term = shifted((cc + i) % 2, k - CONV_TAPS // 2) * cw_ref[k:k + 1, sl]
            acc = term if acc is None else acc + term
        acc = acc + cb_ref[:, sl]
        y = acc * _sigmoid(acc)
        if cc * cw < D_INNER:
            xs_ref[:, sl] = y
        else:
            bc_ref[:, cc * cw - D_INNER:(cc + 1) * cw - D_INNER] = y.astype(BF16)

    step = 256

    def z_part(j):
        zg_ref[:, j * step:(j + 1) * step] = mm(wzx_ref, j * step, (j + 1) * step)

    def gate_part(j):
        zg_ref[:, D_INNER + j * step:D_INNER + (j + 1) * step] = mm(
            wkvg_ref, 2 * KV_WIDTH + j * step, 2 * KV_WIDTH + (j + 1) * step)

    def q_part(j):
        qkv_ref[:, j * step:(j + 1) * step] = mm(wq_ref, j * step, (j + 1) * step).astype(BF16)

    def kv_part(j):
        qkv_ref[:, ATTN_WIDTH + j * step:ATTN_WIDTH + (j + 1) * step] = mm(
            wkvg_ref, j * step, (j + 1) * step).astype(BF16)

    def dt_part():
        dt_ref[...] = mm(wdt_ref, 0, DT_WIDTH)

    plain = ([functools.partial(z_part, j) for j in range(D_INNER // step)]
             + [functools.partial(gate_part, j) for j in range(2 * D_MODEL // step)]
             + [functools.partial(q_part, j) for j in range(ATTN_WIDTH // step)]
             + [functools.partial(kv_part, j) for j in range(2 * KV_WIDTH // step)] + [dt_part])
    n_chunks = CONV_DIM // cw
    project_xbc(0)
    for cc in range(n_chunks):
        if cc + 1 < n_chunks:
            project_xbc(cc + 1)
        conv_silu(cc)
        for job in plain[cc * len(plain) // n_chunks:(cc + 1) * len(plain) // n_chunks]:
            job()


def _inproj(x, g, wzx, wq, wkvg, wdt, conv_w, conv_b, seq, tm=256):
    T = x.shape[0]
    assert seq % tm == 0
    hb = tm // HALO_ROWS
    last_halo = T // HALO_ROWS - 1
    row = lambda i: (i, 0)
    cw = 256
    kern = functools.partial(_inproj_kernel, steps_per_seq=seq // tm, tm=tm, cw=cw)
    return pl.pallas_call(
        kern,
        grid=(T // tm,),
        in_specs=[pl.BlockSpec((HALO_ROWS, D_MODEL), lambda i: (jnp.maximum(i * hb - 1, 0), 0)),
                  pl.BlockSpec((tm, D_MODEL), row),
                  pl.BlockSpec((HALO_ROWS, D_MODEL), lambda i: (jnp.minimum((i + 1) * hb, last_halo), 0)),
                  _resident((1, D_MODEL)), _resident(wzx.shape), _resident(wq.shape), _resident(wkvg.shape),
                  _resident(wdt.shape), _resident((CONV_TAPS, CONV_DIM)), _resident((1, CONV_DIM))],
        out_specs=[pl.BlockSpec((tm, D_INNER), row), pl.BlockSpec((tm, CONV_DIM - D_INNER), row),
                   pl.BlockSpec((tm, ZG_WIDTH), row), pl.BlockSpec((tm, QKV_WIDTH), row),
                   pl.BlockSpec((tm, DT_WIDTH), row)],
        out_shape=[jax.ShapeDtypeStruct((T, D_INNER), F32), jax.ShapeDtypeStruct((T, CONV_DIM - D_INNER), BF16),
                   jax.ShapeDtypeStruct((T, ZG_WIDTH), F32), jax.ShapeDtypeStruct((T, QKV_WIDTH), BF16),
                   jax.ShapeDtypeStruct((T, DT_WIDTH), F32)],
        scratch_shapes=[pltpu.VMEM((2, tm + 2 * HALO_ROWS, cw), F32)],
        compiler_params=_params(),
        name="inproj_conv",
    )(x, x, x, g, wzx, wq, wkvg, wdt, conv_w, conv_b)


_NT = (((1,), (1,)), ((), ()))
_TN = (((0,), (0,)), ((), ()))


def _split3(v):
    hi = v.astype(BF16)
    r1 = v - hi.astype(F32)
    mid = r1.astype(BF16)
    lo = (r1 - mid.astype(F32)).astype(BF16)
    return jnp.concatenate([hi, mid, lo], axis=1)


def _tri_cumsum(adt, upper):
    row = lax.broadcasted_iota(jnp.int32, (CHUNK, CHUNK), 0)
    col = lax.broadcasted_iota(jnp.int32, (CHUNK, CHUNK), 1)
    tri = jnp.where((row <= col) if upper else (row >= col), 1.0, 0.0).astype(BF16)
    p = jnp.dot(tri, _split3(adt), preferred_element_type=F32)
    return p[:, :LANES] + p[:, LANES:2 * LANES] + p[:, 2 * LANES:]


def _pack3(v, lane):
    hi = v.astype(BF16).astype(F32)
    r1 = v - hi
    mid = r1.astype(BF16).astype(F32)
    lo = r1 - mid
    packed = jnp.where(lane < SSD_HEADS, hi,
                       jnp.where(lane < 2 * SSD_HEADS, mid, jnp.where(lane < BWD_LANE, lo, 0.0)))
    return packed.astype(BF16)


def _expand_heads(decay, weight, lane, e_ref):
    packed = jnp.concatenate([_pack3(decay, lane), _pack3(weight, lane)], axis=0)
    ex = jnp.dot(packed, e_ref[...], preferred_element_type=F32)
    return ex[:CHUNK], ex[CHUNK:]


def _ssd_fwd_kernel(xs_ref, bc_ref, dt_ref, dtb_ref, alog_ref, e_ref, y_ref, s_ref, *, steps_per_seq):
    c = pl.program_id(0)

    @pl.when(c % steps_per_seq == 0)
    def _():
        s_ref[...] = jnp.zeros_like(s_ref)

    row = lax.broadcasted_iota(jnp.int32, (CHUNK, CHUNK), 0)
    col = lax.broadcasted_iota(jnp.int32, (CHUNK, CHUNK), 1)
    strict_lower = row > col
    diag = row == col
    lane = lax.broadcasted_iota(jnp.int32, (CHUNK, LANES), 1)
    head_of_lane = lax.broadcasted_iota(jnp.int32, (CHUNK, GROUP_WIDTH), 1) // SSD_HEAD_DIM
    a = -jnp.exp(alog_ref[...])
    nb = SSD_GROUPS * SSD_STATE
    for ci in range(SSD_STEP_CHUNKS):
        rows = slice(ci * CHUNK, (ci + 1) * CHUNK)
        dt = _softplus(dt_ref[rows, :] + dtb_ref[...])
        adt = dt * a
        acs = jnp.where(lane < BWD_LANE, _tri_cumsum(adt, False), _tri_cumsum(adt, True))
        q_t = (acs - jnp.log(dt)).T
        dt_t = dt.T
        log_dt_sum_t = jnp.log(dt_t[0:SSD_HEADS] + dt_t[BWD_LANE:BWD_LANE + SSD_HEADS])
        decay_in, w_state = _expand_heads(jnp.exp(acs), jnp.exp(acs[CHUNK - 1:CHUNK, :] - acs) * dt, lane, e_ref)
        xs = xs_ref[rows, :]
        xw = (xs * w_state).astype(BF16)
        xb = xs.astype(BF16)
        for g in range(SSD_GROUPS):
            gs = slice(g * GROUP_WIDTH, (g + 1) * GROUP_WIDTH)
            b_g = bc_ref[rows, g * SSD_STATE:(g + 1) * SSD_STATE]
            c_g = bc_ref[rows, nb + g * SSD_STATE:nb + (g + 1) * SSD_STATE]
            cb = lax.dot_general(c_g, b_g, _NT, preferred_element_type=F32)
            x_g = xb[:, gs]
            ms, xm = [], []
            for r in range(HEADS_PER_GROUP):
                hf = g * HEADS_PER_GROUP + r
                hb = BWD_LANE + hf
                arg = jnp.where(strict_lower, acs[:, hf:hf + 1] - q_t[hf:hf + 1, :],
                                jnp.where(diag, log_dt_sum_t[hf:hf + 1, :], acs[:, hb:hb + 1] - q_t[hb:hb + 1, :]))
                ms.append((cb * jnp.exp(arg)).astype(BF16))
                xm.append(jnp.where(head_of_lane == r, x_g, jnp.zeros_like(x_g)))
            y_diag = (jnp.dot(jnp.concatenate(ms[0:2], axis=1), jnp.concatenate(xm[0:2], axis=0),
                              preferred_element_type=F32)
                      + jnp.dot(jnp.concatenate(ms[2:4], axis=1), jnp.concatenate(xm[2:4], axis=0),
                                preferred_element_type=F32))
            s_g = s_ref[g]
            y_off = jnp.dot(c_g, s_g.astype(BF16), preferred_element_type=F32) * decay_in[:, gs]
            y_ref[rows, gs] = y_diag + y_off
            upd = lax.dot_general(b_g, xw[:, gs], _TN, preferred_element_type=F32)
            s_ref[g] = s_g * decay_in[CHUNK - 1:CHUNK, gs] + upd


def _ssd_fwd(xs, bc, dt, dtb, alog, e, seq):
    T = xs.shape[0]
    rows = SSD_STEP_CHUNKS * CHUNK
    assert seq % rows == 0
    row = lambda c: (c, 0)
    kern = functools.partial(_ssd_fwd_kernel, steps_per_seq=seq // rows)
    return pl.pallas_call(
        kern,
        grid=(T // rows,),
        in_specs=[pl.BlockSpec((rows, D_INNER), row), pl.BlockSpec((rows, CONV_DIM - D_INNER), row),
                  pl.BlockSpec((rows, LANES), row), pl.BlockSpec((1, LANES), lambda c: (0, 0)),
                  pl.BlockSpec((1, LANES), lambda c: (0, 0)), _resident(e.shape)],
        out_specs=pl.BlockSpec((rows, D_INNER), row),
        out_shape=jax.ShapeDtypeStruct((T, D_INNER), F32),
        scratch_shapes=[pltpu.VMEM((SSD_GROUPS, SSD_STATE, GROUP_WIDTH), F32)],
        compiler_params=_params(),
        name="ssd_fwd",
    )(xs, bc, dt, dtb, alog, e)


def _ssd_bwd_kernel(xs_ref, bc_ref, dt_ref, dtb_ref, alog_ref, e_ref, yp_ref, z_ref, gs_ref, ga_ref, x_ref,
                    dskip_ref, nw_ref, wso_ref, bg_ref, wo_ref, pmn_ref, out_ref, s_ref, yn_ref, *,
                    steps_per_seq):
    c = pl.program_id(0)

    @pl.when(c % steps_per_seq == 0)
    def _():
        s_ref[...] = jnp.zeros_like(s_ref)

    lane = lax.broadcasted_iota(jnp.int32, (CHUNK, LANES), 1)
    a = -jnp.exp(alog_ref[...])
    nb = SSD_GROUPS * SSD_STATE
    for ci in reversed(range(SSD_STEP_CHUNKS)):
        rows = slice(ci * CHUNK, (ci + 1) * CHUNK)
        dt = _softplus(dt_ref[rows, :] + dtb_ref[...])
        acs = _tri_cumsum(dt * a, True)
        decay_in, w_state = _expand_heads(jnp.exp(acs), jnp.exp(acs[0:1, :] - acs) * dt, lane, e_ref)
        xs = xs_ref[rows, :]
        xw = (xs * w_state).astype(BF16)
        for g in range(SSD_GROUPS):
            gs = slice(g * GROUP_WIDTH, (g + 1) * GROUP_WIDTH)
            b_g = bc_ref[rows, g * SSD_STATE:(g + 1) * SSD_STATE]
            c_g = bc_ref[rows, nb + g * SSD_STATE:nb + (g + 1) * SSD_STATE]
            s_g = s_ref[g]
            y_off = jnp.dot(c_g, s_g.astype(BF16), preferred_element_type=F32) * decay_in[:, gs]
            y = yp_ref[rows, gs] + y_off + xs[:, gs] * dskip_ref[:, gs]
            z = z_ref[rows, gs]
            u = y * (z * _sigmoid(z))
            yn_ref[rows, gs] = _rms(u, nw_ref[:, gs]).astype(BF16)
            upd = lax.dot_general(b_g, xw[:, gs], _TN, preferred_element_type=F32)
            s_ref[g] = s_g * decay_in[0:1, gs] + upd
    y_ssd = jnp.dot(yn_ref[...], wso_ref[...], preferred_element_type=F32)
    gate = _sigmoid(gs_ref[...] + bg_ref[...])
    mix = (gate * y_ssd + ga_ref[...]).astype(BF16)
    mixed = jnp.dot(mix, wo_ref[...], preferred_element_type=F32)
    out_ref[...] = x_ref[...] + _rms(mixed, pmn_ref[...])


def _ssd_bwd(xs, bc, dt, dtb, alog, e, yp, zg, ga, x, dskip, nw, wso, bg, wo, pmn, seq):
    T = xs.shape[0]
    rows = SSD_STEP_CHUNKS * CHUNK
    last = T // rows - 1
    rev = lambda c: (last - c, 0)
    kern = functools.partial(_ssd_bwd_kernel, steps_per_seq=seq // rows)
    return pl.pallas_call(
        kern,
        grid=(T // rows,),
        in_specs=[pl.BlockSpec((rows, D_INNER), rev), pl.BlockSpec((rows, CONV_DIM - D_INNER), rev),
                  pl.BlockSpec((rows, LANES), lambda c: (last - c, 1)),
                  pl.BlockSpec((1, LANES), lambda c: (0, 1)), pl.BlockSpec((1, LANES), lambda c: (0, 1)),
                  _resident(e.shape),
                  pl.BlockSpec((rows, D_INNER), rev),
                  pl.BlockSpec((rows, D_INNER), rev),
                  pl.BlockSpec((rows, D_MODEL), lambda c: (last - c, 2)),
                  pl.BlockSpec((rows, D_MODEL), rev), pl.BlockSpec((rows, D_MODEL), rev),
                  _resident((1, D_INNER)), _resident((1, D_INNER)), _resident((D_INNER, D_MODEL)),
                  _resident((1, D_MODEL)), _resident((D_MODEL, D_MODEL)), _resident((1, D_MODEL))],
        out_specs=pl.BlockSpec((rows, D_MODEL), rev),
        out_shape=jax.ShapeDtypeStruct((T, D_MODEL), F32),
        scratch_shapes=[pltpu.VMEM((SSD_GROUPS, SSD_STATE, GROUP_WIDTH), F32), pltpu.VMEM((rows, D_INNER), BF16)],
        compiler_params=_params(),
        name="ssd_bwd_mix",
    )(xs, bc, dt, dtb, alog, e, yp, zg, zg, ga, x, dskip, nw, wso, bg, wo, pmn)


def _attn_kernel(q_ref, kp_ref, kc_ref, kn_ref, vp_ref, vc_ref, vn_ref, bias_ref, sink_ref, gate_ref, bg_ref,
                 wao_ref, out_ref, *, nb):
    n = pl.program_id(0)
    first = (n % nb) == 0
    last = (n % nb) == nb - 1
    kcol = lax.broadcasted_iota(jnp.int32, (1, 3 * ATTN_BLOCK), 1)
    in_seq = jnp.logical_and(jnp.logical_or(kcol >= ATTN_BLOCK, jnp.logical_not(first)),
                             jnp.logical_or(kcol < 2 * ATTN_BLOCK, jnp.logical_not(last)))
    k = jnp.concatenate([kp_ref[...], kc_ref[...], kn_ref[...]], axis=0)
    v = jnp.concatenate([vp_ref[...], vc_ref[...], vn_ref[...]], axis=0)
    low_half = lax.broadcasted_iota(jnp.int32, (ATTN_BLOCK, LANES), 1) < HEAD_DIM
    outs = []
    for t in range(KV_HEADS // 2):
        kt = k[:, t * LANES:(t + 1) * LANES]
        vt = v[:, t * LANES:(t + 1) * LANES]
        qs = []
        for r in range(4):
            qt = q_ref[:, (4 * t + r) * LANES:(4 * t + r + 1) * LANES]
            zero = jnp.zeros_like(qt)
            qs += [jnp.where(low_half, qt, zero), jnp.where(low_half, zero, qt)]
        qz = jnp.concatenate(qs, axis=0)
        s = lax.dot_general(qz, kt, _NT, preferred_element_type=F32) + bias_ref[t]
        s = jnp.where(in_seq, s, -jnp.inf)
        sk = sink_ref[t]
        m = jnp.maximum(jnp.max(s, axis=-1, keepdims=True), sk)
        p = jnp.exp(s - m).astype(BF16)
        o = jnp.dot(p, jnp.concatenate([vt, jnp.ones_like(vt)], axis=1), preferred_element_type=F32)
        den = o[:, LANES:] + jnp.exp(sk - m)
        o = o[:, :LANES] * (1.0 / den)
        for r in range(4):
            lo = o[(2 * r) * ATTN_BLOCK:(2 * r + 1) * ATTN_BLOCK]
            hi = o[(2 * r + 1) * ATTN_BLOCK:(2 * r + 2) * ATTN_BLOCK]
            outs.append(jnp.where(low_half, lo, hi).astype(BF16))
    o = jnp.concatenate(outs, axis=1)
    y = jnp.dot(o, wao_ref[...], preferred_element_type=F32)
    out_ref[...] = _sigmoid(gate_ref[...] + bg_ref[...]) * y


def _attn(qkv, bias, sink, zg, bg, wao, seq):
    T = qkv.shape[0]
    nb = seq // ATTN_BLOCK
    nblk = T // ATTN_BLOCK
    kcol = ATTN_WIDTH // KV_WIDTH
    vcol = kcol + 1
    prev = lambda n: jnp.maximum(n - 1, 0)
    nxt = lambda n: jnp.minimum(n + 1, nblk - 1)
    kern = functools.partial(_attn_kernel, nb=nb)
    return pl.pallas_call(
        kern,
        grid=(nblk,),
        in_specs=[pl.BlockSpec((ATTN_BLOCK, ATTN_WIDTH), lambda n: (n, 0)),
                  pl.BlockSpec((ATTN_BLOCK, KV_WIDTH), lambda n: (prev(n), kcol)),
                  pl.BlockSpec((ATTN_BLOCK, KV_WIDTH), lambda n: (n, kcol)),
                  pl.BlockSpec((ATTN_BLOCK, KV_WIDTH), lambda n: (nxt(n), kcol)),
                  pl.BlockSpec((ATTN_BLOCK, KV_WIDTH), lambda n: (prev(n), vcol)),
                  pl.BlockSpec((ATTN_BLOCK, KV_WIDTH), lambda n: (n, vcol)),
                  pl.BlockSpec((ATTN_BLOCK, KV_WIDTH), lambda n: (nxt(n), vcol)),
                  _resident(bias.shape), _resident(sink.shape),
                  pl.BlockSpec((ATTN_BLOCK, D_MODEL), lambda n: (n, 3)),
                  _resident((1, D_MODEL)), _resident((ATTN_WIDTH, D_MODEL))],
        out_specs=pl.BlockSpec((ATTN_BLOCK, D_MODEL), lambda n: (n, 0)),
        out_shape=jax.ShapeDtypeStruct((T, D_MODEL), F32),
        compiler_params=_params(),
        name="attn",
    )(qkv, qkv, qkv, qkv, qkv, qkv, qkv, bias, sink, zg, bg, wao)


def _mlp_kernel(x_ref, g1_ref, w1_ref, w2_ref, g2_ref, out_ref):
    x = x_ref[...]
    h = _rms(x, g1_ref[...]).astype(BF16)
    step = 1024
    acc = None
    for j in range(D_FF // step):
        a = jnp.dot(h, w1_ref[:, j * step:(j + 1) * step], preferred_element_type=F32)
        a = jnp.square(jnp.maximum(a, 0.0)).astype(BF16)
        t = jnp.dot(a, w2_ref[j * step:(j + 1) * step, :], preferred_element_type=F32)
        acc = t if acc is None else acc + t
    out_ref[...] = x + _rms(acc, g2_ref[...])


def _mlp(x, g1, w1, w2, g2, tm=256):
    T = x.shape[0]
    row = lambda i: (i, 0)
    return pl.pallas_call(
        _mlp_kernel,
        grid=(T // tm,),
        in_specs=[pl.BlockSpec((tm, D_MODEL), row), _resident((1, D_MODEL)), _resident((D_MODEL, D_FF)),
                  _resident((D_FF, D_MODEL)), _resident((1, D_MODEL))],
        out_specs=pl.BlockSpec((tm, D_MODEL), row),
        out_shape=jax.ShapeDtypeStruct((T, D_MODEL), F32),
        compiler_params=_params(),
        name="mlp",
    )(x, g1, w1, w2, g2)


def _permute_heads(a, axis):
    shape = a.shape
    a = a.reshape(shape[:axis] + (KV_HEADS // 2, 2, 4) + shape[axis + 1:])
    a = jnp.swapaxes(a, axis + 1, axis + 2)
    return a.reshape(shape)


def _dt_layout(fwd, bwd):
    return jnp.concatenate([fwd, fwd, fwd, bwd, bwd, bwd, bwd, jnp.zeros_like(bwd)], axis=-1)


def _prep_w_in(w):
    dt0 = D_INNER + CONV_DIM
    q0 = dt0 + 2 * SSD_HEADS
    k0 = q0 + ATTN_WIDTH
    wzx = w[:, :dt0].astype(BF16)
    wq = w[:, q0:k0].reshape(D_MODEL, Q_HEADS, HEAD_DIM) * (HEAD_DIM ** -0.5)
    wq = _permute_heads(wq, 1).reshape(D_MODEL, ATTN_WIDTH).astype(BF16)
    wkvg = w[:, k0:].astype(BF16)
    wdt = _dt_layout(w[:, dt0:dt0 + SSD_HEADS], w[:, dt0 + SSD_HEADS:q0]).astype(BF16)
    return wzx, wq, wkvg, wdt


def _expand_matrix():
    e = np.zeros((LANES, D_INNER), np.float32)
    for p in range(3):
        for h in range(SSD_HEADS):
            e[p * SSD_HEADS + h, h * SSD_HEAD_DIM:(h + 1) * SSD_HEAD_DIM] = 1.0
    return jnp.asarray(e, BF16)


def _t5_bucket(rel):
    nb = N_BUCKETS // 2
    max_exact = nb // 2
    ret = jnp.where(rel > 0, nb, 0)
    n = jnp.abs(rel)
    nf = jnp.maximum(n, 1).astype(F32)
    large = max_exact + (jnp.log(nf / max_exact) / math.log(MAX_DISTANCE / max_exact)
                         * (nb - max_exact)).astype(jnp.int32)
    large = jnp.minimum(large, nb - 1)
    return ret + jnp.where(n < max_exact, n, large)


def _window_bias(rel_table):
    period = 4 * ATTN_BLOCK
    d = jnp.arange(period)
    d = jnp.where(d >= 3 * ATTN_BLOCK, d - period, d)
    rel = d - ATTN_BLOCK
    vec = rel_table[_t5_bucket(rel)].astype(F32)
    vec = jnp.where((jnp.abs(rel) <= WINDOW)[:, None], vec, -jnp.inf).T
    skew = jnp.tile(vec, (1, ATTN_BLOCK))[:, :ATTN_BLOCK * (period - 1)]
    bias = skew.reshape(Q_HEADS, ATTN_BLOCK, period - 1)[:, :, :3 * ATTN_BLOCK]
    return _permute_heads(bias, 0).reshape(KV_HEADS // 2, 8 * ATTN_BLOCK, 3 * ATTN_BLOCK)


def _stack_sink(sink):
    s = _permute_heads(sink.astype(F32), 0)
    return jnp.repeat(s, ATTN_BLOCK).reshape(KV_HEADS // 2, 8 * ATTN_BLOCK, 1)


def kernel(x, pre_mix_norm, w_in, b_gate, conv_w, conv_b, dt_bias, a_log, d_skip, ssd_norm, w_ssd_out, attn_sink,
           rel_bias_table, w_attn_out, w_o, post_mix_norm, pre_mlp_norm, w_mlp_in, w_mlp_out, post_mlp_norm):
    bsz, seq, d = x.shape
    assert d == D_MODEL and seq % (SSD_STEP_CHUNKS * CHUNK) == 0 and seq % ATTN_BLOCK == 0
    depth = w_in.shape[0]
    xf = x.reshape(bsz * seq, d)
    e = _expand_matrix()
    bias = _window_bias(rel_bias_table)
    r1 = lambda v: v.reshape(1, -1)
    for l in range(depth):
        xs, bc, zg, qkv, dtr = _inproj(xf, r1(pre_mix_norm[l]), *_prep_w_in(w_in[l]),
                                       conv_w[l].reshape(CONV_TAPS, CONV_DIM), r1(conv_b[l]), seq)
        dtb = r1(_dt_layout(dt_bias[l, 0], dt_bias[l, 1]).astype(F32))
        alog = r1(_dt_layout(a_log[l, 0], a_log[l, 1]).astype(F32))
        yp = _ssd_fwd(xs, bc, dtr, dtb, alog, e, seq)
        wao = _permute_heads(w_attn_out[l].reshape(Q_HEADS, HEAD_DIM, D_MODEL), 0).reshape(ATTN_WIDTH, D_MODEL)
        ga = _attn(qkv, bias, _stack_sink(attn_sink[l]), zg, r1(b_gate[l, D_MODEL:]), wao.astype(BF16), seq)
        dskip = r1(jnp.repeat(d_skip[l], SSD_HEAD_DIM))
        xf = _ssd_bwd(xs, bc, dtr, dtb, alog, e, yp, zg, ga, xf, dskip, r1(ssd_norm[l]),
                      w_ssd_out[l].astype(BF16), r1(b_gate[l, :D_MODEL]), w_o[l].astype(BF16),
                      r1(post_mix_norm[l]), seq)
        xf = _mlp(xf, r1(pre_mlp_norm[l]), w_mlp_in[l].astype(BF16), w_mlp_out[l].astype(BF16),
                  r1(post_mlp_norm[l]))
    return xf.reshape(bsz, seq, d)
```

```python
import functools
import math

import numpy as np
import jax
import jax.numpy as jnp
from jax import lax
from jax.experimental import pallas as pl
from jax.experimental.pallas import tpu as pltpu

F32 = jnp.float32
BF16 = jnp.bfloat16
EPS = 1e-6

D_MODEL = 1024
D_INNER = 2048
SSD_HEADS = 32
SSD_HEAD_DIM = 64
SSD_GROUPS = 8
HEADS_PER_GROUP = SSD_HEADS // SSD_GROUPS
GROUP_WIDTH = D_INNER // SSD_GROUPS
SSD_STATE = 128
CHUNK = 128
CONV_TAPS = 5
CONV_DIM = D_INNER + 2 * SSD_GROUPS * SSD_STATE
Q_HEADS = 16
KV_HEADS = 4
HEAD_DIM = 64
ATTN_WIDTH = Q_HEADS * HEAD_DIM
KV_WIDTH = KV_HEADS * HEAD_DIM
WINDOW = 128
ATTN_BLOCK = 128
N_BUCKETS = 32
MAX_DISTANCE = 128
D_FF = 4 * D_MODEL
LANES = 128
HALO_ROWS = 8
DT_WIDTH = 2 * LANES
BWD_LANE = 3 * SSD_HEADS

ZG_WIDTH = D_INNER + 2 * D_MODEL
QKV_WIDTH = ATTN_WIDTH + 2 * KV_WIDTH
SSD_STEP_CHUNKS = 2
ATTN_STEP_BLOCKS = 2

VMEM_LIMIT = 56 * 1024 * 1024


def _params(n_axes=1, flags=None):
    return pltpu.CompilerParams(dimension_semantics=("arbitrary",) * n_axes,
                                vmem_limit_bytes=VMEM_LIMIT, flags=flags)


def _resident(shape):
    nd = len(shape)
    return pl.BlockSpec(shape, lambda *_: (0,) * nd, pipeline_mode=pl.Buffered(1))


_NT = (((1,), (1,)), ((), ()))
_TN = (((0,), (0,)), ((), ()))


def _rms(x, g):
    ms = jnp.mean(x * x, axis=-1, keepdims=True)
    return x * lax.rsqrt(ms + EPS) * g


def _sigmoid(x):
    return 1.0 / (1.0 + jnp.exp(-x))


def _softplus(x):
    e = jnp.exp(-jnp.abs(x))
    u = 1.0 + e
    log1p_e = jnp.where(u == 1.0, e, jnp.log(u) * (e / (u - 1.0)))
    return jnp.maximum(x, 0.0) + log1p_e


def _inproj_kernel(xp_ref, x_ref, xn_ref, g_ref, wzx_ref, wq_ref, wkvg_ref, wdt_ref, cw_ref, cb_ref,
                   xs_ref, bc_ref, zg_ref, qkv_ref, dt_ref, ext_ref, *, steps_per_seq, tm, cw):
    i = pl.program_id(0)
    first = (i % steps_per_seq) == 0
    last = (i % steps_per_seq) == steps_per_seq - 1
    g = g_ref[...]
    x = x_ref[...]
    h = _rms(x, g).astype(BF16)
    h_ext = _rms(jnp.concatenate([xp_ref[...], x, xn_ref[...]], axis=0), g).astype(BF16)

    def mm(w_ref, lo, hi):
        return lax.dot_general(h, w_ref[lo:hi, :], _NT, preferred_element_type=F32)

    sub = lax.broadcasted_iota(jnp.int32, (tm, cw), 0) % HALO_ROWS

    def shifted(slot, d):
        cur = ext_ref[slot, HALO_ROWS:HALO_ROWS + tm, :]
        if d == 0:
            return cur
        if d < 0:
            blend = jnp.where(sub < HALO_ROWS + d, cur, ext_ref[slot, 0:tm, :])
            rot = -d
        else:
            blend = jnp.where(sub >= d, cur, ext_ref[slot, 2 * HALO_ROWS:2 * HALO_ROWS + tm, :])
            rot = HALO_ROWS - d
        blend = blend.reshape(tm // HALO_ROWS, HALO_ROWS, cw)
        return pltpu.roll(blend, rot, axis=1).reshape(tm, cw)

    def project_xbc(cc):
        slot = (cc + i) % 2
        ext = lax.dot_general(h_ext, wzx_ref[D_INNER + cc * cw:D_INNER + (cc + 1) * cw, :], _NT,
                              preferred_element_type=F32)
        ext_ref[slot, 0:HALO_ROWS, :] = jnp.where(first, 0.0, ext[:HALO_ROWS])
        ext_ref[slot, HALO_ROWS:HALO_ROWS + tm, :] = ext[HALO_ROWS:HALO_ROWS + tm]
        ext_ref[slot, HALO_ROWS + tm:, :] = jnp.where(last, 0.0, ext[HALO_ROWS + tm:])

    def conv_silu(cc):
        sl = slice(cc * cw, (cc + 1) * cw)
        acc = None
        for k in range(CONV_TAPS):
            term = shifted((cc + i) % 2, k - CONV_TAPS // 2) * cw_ref[k:k + 1, sl]
            acc = term if acc is None else acc + term
        acc = acc + cb_ref[:, sl]
        y = acc * _sigmoid(acc)
        if cc * cw < D_INNER:
            xs_ref[:, sl] = y
        else:
            bc_ref[:, cc * cw - D_INNER:(cc + 1) * cw - D_INNER] = y.astype(BF16)

    step = 256

    def z_part(j):
        zg_ref[:, j * step:(j + 1) * step] = mm(wzx_ref, j * step, (j + 1) * step)

    def gate_part(j):
        zg_ref[:, D_INNER + j * step:D_INNER + (j + 1) * step] = mm(
            wkvg_ref, 2 * KV_WIDTH + j * step, 2 * KV_WIDTH + (j + 1) * step)

    def q_part(j):
        qkv_ref[:, j * step:(j + 1) * step] = mm(wq_ref, j * step, (j + 1) * step).astype(BF16)

    def kv_part(j):
        qkv_ref[:, ATTN_WIDTH + j * step:ATTN_WIDTH + (j + 1) * step] = mm(
            wkvg_ref, j * step, (j + 1) * step).astype(BF16)

    def dt_part():
        dt_ref[...] = mm(wdt_ref, 0, DT_WIDTH)

    plain = ([functools.partial(z_part, j) for j in range(D_INNER // step)]
             + [functools.partial(gate_part, j) for j in range(2 * D_MODEL // step)]
             + [functools.partial(q_part, j) for j in range(ATTN_WIDTH // step)]
             + [functools.partial(kv_part, j) for j in range(2 * KV_WIDTH // step)] + [dt_part])
    n_chunks = CONV_DIM // cw
    project_xbc(0)
    for cc in range(n_chunks):
        if cc + 1 < n_chunks:
            project_xbc(cc + 1)
        conv_silu(cc)
        for job in plain[cc * len(plain) // n_chunks:(cc + 1) * len(plain) // n_chunks]:
            job()


def _inproj(x, g, wzx, wq, wkvg, wdt, conv_w, conv_b, seq, tm=256):
    T = x.shape[0]
    assert seq % tm == 0
    hb = tm // HALO_ROWS
    last_halo = T // HALO_ROWS - 1
    row = lambda i: (i, 0)
    cw = 256
    kern = functools.partial(_inproj_kernel, steps_per_seq=seq // tm, tm=tm, cw=cw)
    return pl.pallas_call(
        kern,
        grid=(T // tm,),
        in_specs=[pl.BlockSpec((HALO_ROWS, D_MODEL), lambda i: (jnp.maximum(i * hb - 1, 0), 0)),
                  pl.BlockSpec((tm, D_MODEL), row),
                  pl.BlockSpec((HALO_ROWS, D_MODEL), lambda i: (jnp.minimum((i + 1) * hb, last_halo), 0)),
                  _resident((1, D_MODEL)), _resident(wzx.shape), _resident(wq.shape), _resident(wkvg.shape),
                  _resident(wdt.shape), _resident((CONV_TAPS, CONV_DIM)), _resident((1, CONV_DIM))],
        out_specs=[pl.BlockSpec((tm, D_INNER), row), pl.BlockSpec((tm, CONV_DIM - D_INNER), row),
                   pl.BlockSpec((tm, ZG_WIDTH), row), pl.BlockSpec((tm, QKV_WIDTH), row),
                   pl.BlockSpec((tm, DT_WIDTH), row)],
        out_shape=[jax.ShapeDtypeStruct((T, D_INNER), F32), jax.ShapeDtypeStruct((T, CONV_DIM - D_INNER), BF16),
                   jax.ShapeDtypeStruct((T, ZG_WIDTH), F32), jax.ShapeDtypeStruct((T, QKV_WIDTH), BF16),
                   jax.ShapeDtypeStruct((T, DT_WIDTH), F32)],
        scratch_shapes=[pltpu.VMEM((2, tm + 2 * HALO_ROWS, cw), F32)],
        compiler_params=_params(),
        name="inproj_conv",
    )(x, x, x, g, wzx, wq, wkvg, wdt, conv_w, conv_b)


def _split3(v):
    hi = v.astype(BF16)
    r1 = v - hi.astype(F32)
    mid = r1.astype(BF16)
    lo = (r1 - mid.astype(F32)).astype(BF16)
    return jnp.concatenate([hi, mid, lo], axis=1)


def _tri_cumsum(adt, upper):
    row = lax.broadcasted_iota(jnp.int32, (CHUNK, CHUNK), 0)
    col = lax.broadcasted_iota(jnp.int32, (CHUNK, CHUNK), 1)
    tri = jnp.where((row <= col) if upper else (row >= col), 1.0, 0.0).astype(BF16)
    p = jnp.dot(tri, _split3(adt), preferred_element_type=F32)
    return p[:, :LANES] + p[:, LANES:2 * LANES] + p[:, 2 * LANES:]


def _pack3(v, lane):
    hi = v.astype(BF16).astype(F32)
    r1 = v - hi
    mid = r1.astype(BF16).astype(F32)
    lo = r1 - mid
    packed = jnp.where(lane < SSD_HEADS, hi,
                       jnp.where(lane < 2 * SSD_HEADS, mid, jnp.where(lane < BWD_LANE, lo, 0.0)))
    return packed.astype(BF16)


def _expand_heads(decay, weight, lane, e_ref):
    packed = jnp.concatenate([_pack3(decay, lane), _pack3(weight, lane)], axis=0)
    ex = jnp.dot(packed, e_ref[...], preferred_element_type=F32)
    return ex[:CHUNK], ex[CHUNK:]


def _ssd_fwd_kernel(xs_ref, bc_ref, dt_ref, dtb_ref, alog_ref, e_ref, y_ref, s_ref, *, steps_per_seq):
    c = pl.program_id(0)

    @pl.when(c % steps_per_seq == 0)
    def _():
        s_ref[...] = jnp.zeros_like(s_ref)

    row = lax.broadcasted_iota(jnp.int32, (CHUNK, CHUNK), 0)
    col = lax.broadcasted_iota(jnp.int32, (CHUNK, CHUNK), 1)
    strict_lower = row > col
    diag = row == col
    lane = lax.broadcasted_iota(jnp.int32, (CHUNK, LANES), 1)
    head_of_lane = lax.broadcasted_iota(jnp.int32, (CHUNK, GROUP_WIDTH), 1) // SSD_HEAD_DIM
    a = -jnp.exp(alog_ref[...])
    nb = SSD_GROUPS * SSD_STATE
    for ci in range(SSD_STEP_CHUNKS):
        rows = slice(ci * CHUNK, (ci + 1) * CHUNK)
        dt = _softplus(dt_ref[rows, :] + dtb_ref[...])
        adt = dt * a
        acs = jnp.where(lane < BWD_LANE, _tri_cumsum(adt, False), _tri_cumsum(adt, True))
        q_t = (acs - jnp.log(dt)).T
        dt_t = dt.T
        log_dt_sum_t = jnp.log(dt_t[0:SSD_HEADS] + dt_t[BWD_LANE:BWD_LANE + SSD_HEADS])
        decay_in, w_state = _expand_heads(jnp.exp(acs), jnp.exp(acs[CHUNK - 1:CHUNK, :] - acs) * dt, lane, e_ref)
        xs = xs_ref[rows, :]
        xw = (xs * w_state).astype(BF16)
        xb = xs.astype(BF16)
        for g in range(SSD_GROUPS):
            gs = slice(g * GROUP_WIDTH, (g + 1) * GROUP_WIDTH)
            b_g = bc_ref[rows, g * SSD_STATE:(g + 1) * SSD_STATE]
            c_g = bc_ref[rows, nb + g * SSD_STATE:nb + (g + 1) * SSD_STATE]
            cb = lax.dot_general(c_g, b_g, _NT, preferred_element_type=F32)
            x_g = xb[:, gs]
            ms, xm = [], []
            for r in range(HEADS_PER_GROUP):
                hf = g * HEADS_PER_GROUP + r
                hb = BWD_LANE + hf
                arg = jnp.where(strict_lower, acs[:, hf:hf + 1] - q_t[hf:hf + 1, :],
                                jnp.where(diag, log_dt_sum_t[hf:hf + 1, :], acs[:, hb:hb + 1] - q_t[hb:hb + 1, :]))
                ms.append((cb * jnp.exp(arg)).astype(BF16))
                xm.append(jnp.where(head_of_lane == r, x_g, jnp.zeros_like(x_g)))
            y_diag = (jnp.dot(jnp.concatenate(ms[0:2], axis=1), jnp.concatenate(xm[0:2], axis=0),
                              preferred_element_type=F32)
                      + jnp.dot(jnp.concatenate(ms[2:4], axis=1), jnp.concatenate(xm[2:4], axis=0),
                                preferred_element_type=F32))
            s_g = s_ref[g]
            y_off = jnp.dot(c_g, s_g.astype(BF16), preferred_element_type=F32) * decay_in[:, gs]
            y_ref[rows, gs] = y_diag + y_off
            upd = lax.dot_general(b_g, xw[:, gs], _TN, preferred_element_type=F32)
            s_ref[g] = s_g * decay_in[CHUNK - 1:CHUNK, gs] + upd


def _ssd_fwd(xs, bc, dt, dtb, alog, e, seq):
    T = xs.shape[0]
    rows = SSD_STEP_CHUNKS * CHUNK
    assert seq % rows == 0
    row = lambda c: (c, 0)
    kern = functools.partial(_ssd_fwd_kernel, steps_per_seq=seq // rows)
    return pl.pallas_call(
        kern,
        grid=(T // rows,),
        in_specs=[pl.BlockSpec((rows, D_INNER), row), pl.BlockSpec((rows, CONV_DIM - D_INNER), row),
                  pl.BlockSpec((rows, LANES), row), pl.BlockSpec((1, LANES), lambda c: (0, 0)),
                  pl.BlockSpec((1, LANES), lambda c: (0, 0)), _resident(e.shape)],
        out_specs=pl.BlockSpec((rows, D_INNER), row),
        out_shape=jax.ShapeDtypeStruct((T, D_INNER), F32),
        scratch_shapes=[pltpu.VMEM((SSD_GROUPS, SSD_STATE, GROUP_WIDTH), F32)],
        compiler_params=_params(),
        name="ssd_fwd",
    )(xs, bc, dt, dtb, alog, e)


def _ssd_bwd_kernel(xs_ref, bc_ref, dt_ref, dtb_ref, alog_ref, e_ref, yp_ref, z_ref, gs_ref, ga_ref, x_ref,
                    dskip_ref, nw_ref, wso_ref, bg_ref, wo_ref, pmn_ref, out_ref, s_ref, yn_ref, *,
                    steps_per_seq):
    c = pl.program_id(0)

    @pl.when(c % steps_per_seq == 0)
    def _():
        s_ref[...] = jnp.zeros_like(s_ref)

    lane = lax.broadcasted_iota(jnp.int32, (CHUNK, LANES), 1)
    a = -jnp.exp(alog_ref[...])
    nb = SSD_GROUPS * SSD_STATE
    for ci in reversed(range(SSD_STEP_CHUNKS)):
        rows = slice(ci * CHUNK, (ci + 1) * CHUNK)
        dt = _softplus(dt_ref[rows, :] + dtb_ref[...])
        acs = _tri_cumsum(dt * a, True)
        decay_in, w_state = _expand_heads(jnp.exp(acs), jnp.exp(acs[0:1, :] - acs) * dt, lane, e_ref)
        xs = xs_ref[rows, :]
        xw = (xs * w_state).astype(BF16)
        for g in range(SSD_GROUPS):
            gs = slice(g * GROUP_WIDTH, (g + 1) * GROUP_WIDTH)
            b_g = bc_ref[rows, g * SSD_STATE:(g + 1) * SSD_STATE]
            c_g = bc_ref[rows, nb + g * SSD_STATE:nb + (g + 1) * SSD_STATE]
            s_g = s_ref[g]
            y_off = jnp.dot(c_g, s_g.astype(BF16), preferred_element_type=F32) * decay_in[:, gs]
            y = yp_ref[rows, gs] + y_off + xs[:, gs] * dskip_ref[:, gs]
            z = z_ref[rows, gs]
            u = y * (z * _sigmoid(z))
            yn_ref[rows, gs] = _rms(u, nw_ref[:, gs]).astype(BF16)
            upd = lax.dot_general(b_g, xw[:, gs], _TN, preferred_element_type=F32)
            s_ref[g] = s_g * decay_in[0:1, gs] + upd
    y_ssd = jnp.dot(yn_ref[...], wso_ref[...], preferred_element_type=F32)
    gate = _sigmoid(gs_ref[...] + bg_ref[...])
    mix = (gate * y_ssd + ga_ref[...]).astype(BF16)
    mixed = jnp.dot(mix, wo_ref[...], preferred_element_type=F32)
    out_ref[...] = x_ref[...] + _rms(mixed, pmn_ref[...])


def _ssd_bwd(xs, bc, dt, dtb, alog, e, yp, zg, ga, x, dskip, nw, wso, bg, wo, pmn, seq):
    T = xs.shape[0]
    rows = SSD_STEP_CHUNKS * CHUNK
    last = T // rows - 1
    rev = lambda c: (last - c, 0)
    kern = functools.partial(_ssd_bwd_kernel, steps_per_seq=seq // rows)
    return pl.pallas_call(
        kern,
        grid=(T // rows,),
        in_specs=[pl.BlockSpec((rows, D_INNER), rev), pl.BlockSpec((rows, CONV_DIM - D_INNER), rev),
                  pl.BlockSpec((rows, LANES), lambda c: (last - c, 1)),
                  pl.BlockSpec((1, LANES), lambda c: (0, 1)), pl.BlockSpec((1, LANES), lambda c: (0, 1)),
                  _resident(e.shape),
                  pl.BlockSpec((rows, D_INNER), rev),
                  pl.BlockSpec((rows, D_INNER), rev),
                  pl.BlockSpec((rows, D_MODEL), lambda c: (last - c, 2)),
                  pl.BlockSpec((rows, D_MODEL), rev), pl.BlockSpec((rows, D_MODEL), rev),
                  _resident((1, D_INNER)), _resident((1, D_INNER)), _resident((D_INNER, D_MODEL)),
                  _resident((1, D_MODEL)), _resident((D_MODEL, D_MODEL)), _resident((1, D_MODEL))],
        out_specs=pl.BlockSpec((rows, D_MODEL), rev),
        out_shape=jax.ShapeDtypeStruct((T, D_MODEL), F32),
        scratch_shapes=[pltpu.VMEM((SSD_GROUPS, SSD_STATE, GROUP_WIDTH), F32), pltpu.VMEM((rows, D_INNER), BF16)],
        compiler_params=_params(),
        name="ssd_bwd_mix",
    )(xs, bc, dt, dtb, alog, e, yp, zg, zg, ga, x, dskip, nw, wso, bg, wo, pmn)


def _attn_kernel(q_ref, kp_ref, kc_ref, kn_ref, vp_ref, vc_ref, vn_ref, bias_ref, sink_ref, gate_ref, bg_ref,
                 wao_ref, out_ref, *, nb):
    n = pl.program_id(0)
    kcol = lax.broadcasted_iota(jnp.int32, (1, 3 * ATTN_BLOCK), 1)
    k_all = jnp.concatenate([kp_ref[...], kc_ref[...], kn_ref[...]], axis=0)
    v_all = jnp.concatenate([vp_ref[...], vc_ref[...], vn_ref[...]], axis=0)
    low_half = lax.broadcasted_iota(jnp.int32, (ATTN_BLOCK, LANES), 1) < HEAD_DIM
    n_tiles = KV_HEADS // 2
    rep = Q_HEADS // KV_HEADS
    half_rows = rep * ATTN_BLOCK

    def head_group(j, t, e):
        blk = n * ATTN_STEP_BLOCKS + j
        first = (blk % nb) == 0
        last = (blk % nb) == nb - 1
        in_seq = jnp.logical_and(jnp.logical_or(kcol >= ATTN_BLOCK, jnp.logical_not(first)),
                                 jnp.logical_or(kcol < 2 * ATTN_BLOCK, jnp.logical_not(last)))
        kt = k_all[j * ATTN_BLOCK:(j + 3) * ATTN_BLOCK, t * LANES:(t + 1) * LANES]
        vt = v_all[j * ATTN_BLOCK:(j + 3) * ATTN_BLOCK, t * LANES:(t + 1) * LANES]
        keep = low_half if e == 0 else jnp.logical_not(low_half)
        qs = []
        for r in range(rep):
            qt = q_ref[j * ATTN_BLOCK:(j + 1) * ATTN_BLOCK, (rep * t + r) * LANES:(rep * t + r + 1) * LANES]
            qs.append(jnp.where(keep, qt, jnp.zeros_like(qt)))
        qz = jnp.concatenate(qs, axis=0)
        hrows = slice(e * half_rows, (e + 1) * half_rows)
        s = lax.dot_general(qz, kt, _NT, preferred_element_type=F32) + bias_ref[t, hrows, :]
        s = jnp.where(in_seq, s, -jnp.inf)
        sk = sink_ref[t, hrows, :]
        m = jnp.maximum(jnp.max(s, axis=-1, keepdims=True), sk)
        p = jnp.exp(s - m).astype(BF16)
        o = jnp.dot(p, jnp.concatenate([vt, jnp.ones_like(vt)], axis=1), preferred_element_type=F32)
        return o[:, :LANES] * (1.0 / (o[:, LANES:] + jnp.exp(sk - m)))

    block_outs = []
    for j in range(ATTN_STEP_BLOCKS):
        tiles = []
        for t in range(n_tiles):
            lo, hi = head_group(j, t, 0), head_group(j, t, 1)
            for r in range(rep):
                rr = slice(r * ATTN_BLOCK, (r + 1) * ATTN_BLOCK)
                tiles.append(jnp.where(low_half, lo[rr], hi[rr]).astype(BF16))
        block_outs.append(jnp.concatenate(tiles, axis=1))
    o = jnp.concatenate(block_outs, axis=0)
    y = jnp.dot(o, wao_ref[...], preferred_element_type=F32)
    out_ref[...] = _sigmoid(gate_ref[...] + bg_ref[...]) * y


def _attn(qkv, bias, sink, zg, bg, wao, seq):
    T = qkv.shape[0]
    nb = seq // ATTN_BLOCK
    nblk = T // ATTN_BLOCK
    rows = ATTN_STEP_BLOCKS * ATTN_BLOCK
    assert seq % rows == 0
    kcol = ATTN_WIDTH // KV_WIDTH
    vcol = kcol + 1
    prev = lambda n: jnp.maximum(n * ATTN_STEP_BLOCKS - 1, 0)
    nxt = lambda n: jnp.minimum((n + 1) * ATTN_STEP_BLOCKS, nblk - 1)
    kern = functools.partial(_attn_kernel, nb=nb)
    return pl.pallas_call(
        kern,
        grid=(T // rows,),
        in_specs=[pl.BlockSpec((rows, ATTN_WIDTH), lambda n: (n, 0)),
                  pl.BlockSpec((ATTN_BLOCK, KV_WIDTH), lambda n: (prev(n), kcol)),
                  pl.BlockSpec((rows, KV_WIDTH), lambda n: (n, kcol)),
                  pl.BlockSpec((ATTN_BLOCK, KV_WIDTH), lambda n: (nxt(n), kcol)),
                  pl.BlockSpec((ATTN_BLOCK, KV_WIDTH), lambda n: (prev(n), vcol)),
                  pl.BlockSpec((rows, KV_WIDTH), lambda n: (n, vcol)),
                  pl.BlockSpec((ATTN_BLOCK, KV_WIDTH), lambda n: (nxt(n), vcol)),
                  _resident(bias.shape), _resident(sink.shape),
                  pl.BlockSpec((rows, D_MODEL), lambda n: (n, 3)),
                  _resident((1, D_MODEL)), _resident((ATTN_WIDTH, D_MODEL))],
        out_specs=pl.BlockSpec((rows, D_MODEL), lambda n: (n, 0)),
        out_shape=jax.ShapeDtypeStruct((T, D_MODEL), F32),
        compiler_params=_params(),
        name="attn",
    )(qkv, qkv, qkv, qkv, qkv, qkv, qkv, bias, sink, zg, bg, wao)


def _mlp_kernel(x_ref, g1_ref, w1_ref, w2_ref, g2_ref, out_ref):
    x = x_ref[...]
    h = _rms(x, g1_ref[...]).astype(BF16)
    step = 1024
    acc = None
    for j in range(D_FF // step):
        a = jnp.dot(h, w1_ref[:, j * step:(j + 1) * step], preferred_element_type=F32)
        a = jnp.square(jnp.maximum(a, 0.0)).astype(BF16)
        t = jnp.dot(a, w2_ref[j * step:(j + 1) * step, :], preferred_element_type=F32)
        acc = t if acc is None else acc + t
    out_ref[...] = x + _rms(acc, g2_ref[...])


def _mlp(x, g1, w1, w2, g2, tm=256):
    T = x.shape[0]
    row = lambda i: (i, 0)
    return pl.pallas_call(
        _mlp_kernel,
        grid=(T // tm,),
        in_specs=[pl.BlockSpec((tm, D_MODEL), row), _resident((1, D_MODEL)), _resident((D_MODEL, D_FF)),
                  _resident((D_FF, D_MODEL)), _resident((1, D_MODEL))],
        out_specs=pl.BlockSpec((tm, D_MODEL), row),
        out_shape=jax.ShapeDtypeStruct((T, D_MODEL), F32),
        compiler_params=_params(),
        name="mlp",
    )(x, g1, w1, w2, g2)


def _permute_heads(a, axis):
    shape = a.shape
    a = a.reshape(shape[:axis] + (KV_HEADS // 2, 2, 4) + shape[axis + 1:])
    a = jnp.swapaxes(a, axis + 1, axis + 2)
    return a.reshape(shape)


def _dt_layout(fwd, bwd, axis=-1):
    return jnp.concatenate([fwd, fwd, fwd, bwd, bwd, bwd, bwd, jnp.zeros_like(bwd)], axis=axis)


def _prep_w_in(w):
    wt = w.T
    dt0 = D_INNER + CONV_DIM
    q0 = dt0 + 2 * SSD_HEADS
    k0 = q0 + ATTN_WIDTH
    wzx = wt[:dt0].astype(BF16)
    wq = wt[q0:k0].reshape(Q_HEADS, HEAD_DIM, D_MODEL) * (HEAD_DIM ** -0.5)
    wq = _permute_heads(wq, 0).reshape(ATTN_WIDTH, D_MODEL).astype(BF16)
    wkvg = wt[k0:].astype(BF16)
    wdt = _dt_layout(wt[dt0:dt0 + SSD_HEADS], wt[dt0 + SSD_HEADS:q0], axis=0).astype(BF16)
    return wzx, wq, wkvg, wdt


def _expand_matrix():
    e = np.zeros((LANES, D_INNER), np.float32)
    for p in range(3):
        for h in range(SSD_HEADS):
            e[p * SSD_HEADS + h, h * SSD_HEAD_DIM:(h + 1) * SSD_HEAD_DIM] = 1.0
    return jnp.asarray(e, BF16)


def _t5_bucket(rel):
    nb = N_BUCKETS // 2
    max_exact = nb // 2
    ret = jnp.where(rel > 0, nb, 0)
    n = jnp.abs(rel)
    nf = jnp.maximum(n, 1).astype(F32)
    large = max_exact + (jnp.log(nf / max_exact) / math.log(MAX_DISTANCE / max_exact)
                         * (nb - max_exact)).astype(jnp.int32)
    large = jnp.minimum(large, nb - 1)
    return ret + jnp.where(n < max_exact, n, large)


def _window_bias(rel_table):
    period = 4 * ATTN_BLOCK
    d = jnp.arange(period)
    d = jnp.where(d >= 3 * ATTN_BLOCK, d - period, d)
    rel = d - ATTN_BLOCK
    vec = rel_table[_t5_bucket(rel)].astype(F32)
    vec = jnp.where((jnp.abs(rel) <= WINDOW)[:, None], vec, -jnp.inf).T
    skew = jnp.tile(vec, (1, ATTN_BLOCK))[:, :ATTN_BLOCK * (period - 1)]
    bias = skew.reshape(Q_HEADS, ATTN_BLOCK, period - 1)[:, :, :3 * ATTN_BLOCK]
    return bias.reshape(KV_HEADS // 2, 8 * ATTN_BLOCK, 3 * ATTN_BLOCK)


def _stack_sink(sink):
    return jnp.repeat(sink.astype(F32), ATTN_BLOCK).reshape(KV_HEADS // 2, 8 * ATTN_BLOCK, 1)


def kernel(x, pre_mix_norm, w_in, b_gate, conv_w, conv_b, dt_bias, a_log, d_skip, ssd_norm, w_ssd_out, attn_sink,
           rel_bias_table, w_attn_out, w_o, post_mix_norm, pre_mlp_norm, w_mlp_in, w_mlp_out, post_mlp_norm):
    bsz, seq, d = x.shape
    assert d == D_MODEL and seq % (SSD_STEP_CHUNKS * CHUNK) == 0 and seq % ATTN_BLOCK == 0
    depth = w_in.shape[0]
    xf = x.reshape(bsz * seq, d)
    e = _expand_matrix()
    bias = _window_bias(rel_bias_table)
    r1 = lambda v: v.reshape(1, -1)
    for l in range(depth):
        xs, bc, zg, qkv, dtr = _inproj(xf, r1(pre_mix_norm[l]), *_prep_w_in(w_in[l]),
                                       conv_w[l].reshape(CONV_TAPS, CONV_DIM), r1(conv_b[l]), seq)
        dtb = r1(_dt_layout(dt_bias[l, 0], dt_bias[l, 1]).astype(F32))
        alog = r1(_dt_layout(a_log[l, 0], a_log[l, 1]).astype(F32))
        yp = _ssd_fwd(xs, bc, dtr, dtb, alog, e, seq)
        wao = _permute_heads(w_attn_out[l].reshape(Q_HEADS, HEAD_DIM, D_MODEL), 0).reshape(ATTN_WIDTH, D_MODEL)
        ga = _attn(qkv, bias, _stack_sink(attn_sink[l]), zg, r1(b_gate[l, D_MODEL:]), wao.astype(BF16), seq)
        dskip = r1(jnp.repeat(d_skip[l], SSD_HEAD_DIM))
        xf = _ssd_bwd(xs, bc, dtr, dtb, alog, e, yp, zg, ga, xf, dskip, r1(ssd_norm[l]),
                      w_ssd_out[l].astype(BF16), r1(b_gate[l, :D_MODEL]), w_o[l].astype(BF16),
                      r1(post_mix_norm[l]), seq)
        xf = _mlp(xf, r1(pre_mlp_norm[l]), w_mlp_in[l].astype(BF16), w_mlp_out[l].astype(BF16),
                  r1(post_mlp_norm[l]))
    return xf.reshape(bsz, seq, d)
```

```python
import functools
import math

import numpy as np
import jax
import jax.numpy as jnp
from jax import lax
from jax.experimental import pallas as pl
from jax.experimental.pallas import tpu as pltpu

F32 = jnp.float32
BF16 = jnp.bfloat16
EPS = 1e-6

D_MODEL = 1024
D_INNER = 2048
SSD_HEADS = 32
SSD_HEAD_DIM = 64
SSD_GROUPS = 8
HEADS_PER_GROUP = SSD_HEADS // SSD_GROUPS
GROUP_WIDTH = D_INNER // SSD_GROUPS
SSD_STATE = 128
CHUNK = 128
CONV_TAPS = 5
CONV_DIM = D_INNER + 2 * SSD_GROUPS * SSD_STATE
Q_HEADS = 16
KV_HEADS = 4
HEAD_DIM = 64
ATTN_WIDTH = Q_HEADS * HEAD_DIM
KV_WIDTH = KV_HEADS * HEAD_DIM
WINDOW = 128
ATTN_BLOCK = 128
N_BUCKETS = 32
MAX_DISTANCE = 128
D_FF = 4 * D_MODEL
LANES = 128
HALO_ROWS = 8
DT_WIDTH = 2 * LANES
BWD_LANE = 3 * SSD_HEADS

ZG_WIDTH = D_INNER + 2 * D_MODEL
QKV_WIDTH = ATTN_WIDTH + 2 * KV_WIDTH
SSD_STEP_CHUNKS = 2
ATTN_STEP_BLOCKS = 2

VMEM_LIMIT = 56 * 1024 * 1024


def _params(n_axes=1, flags=None):
    return pltpu.CompilerParams(dimension_semantics=("arbitrary",) * n_axes,
                                vmem_limit_bytes=VMEM_LIMIT, flags=flags)


def _resident(shape):
    nd = len(shape)
    return pl.BlockSpec(shape, lambda *_: (0,) * nd, pipeline_mode=pl.Buffered(1))


_NT = (((1,), (1,)), ((), ()))
_TN = (((0,), (0,)), ((), ()))


def _rms(x, g):
    ms = jnp.mean(x * x, axis=-1, keepdims=True)
    return x * lax.rsqrt(ms + EPS) * g


def _sigmoid(x):
    return 1.0 / (1.0 + jnp.exp(-x))


def _softplus(x):
    e = jnp.exp(-jnp.abs(x))
    u = 1.0 + e
    log1p_e = jnp.where(u == 1.0, e, jnp.log(u) * (e / (u - 1.0)))
    return jnp.maximum(x, 0.0) + log1p_e


def _inproj_kernel(xp_ref, x_ref, xn_ref, g_ref, wzx_ref, wq_ref, wkvg_ref, wdt_ref, cw_ref, cb_ref,
                   xs_ref, bc_ref, zg_ref, qkv_ref, dt_ref, ext_ref, *, steps_per_seq, tm, cw):
    i = pl.program_id(0)
    first = (i % steps_per_seq) == 0
    last = (i % steps_per_seq) == steps_per_seq - 1
    g = g_ref[...]
    x = x_ref[...]
    h = _rms(x, g).astype(BF16)
    h_ext = _rms(jnp.concatenate([xp_ref[...], x, xn_ref[...]], axis=0), g).astype(BF16)

    def mm(w_ref, lo, hi):
        return jnp.dot(h, w_ref[:, lo:hi], preferred_element_type=F32)

    sub = lax.broadcasted_iota(jnp.int32, (tm, cw), 0) % HALO_ROWS

    def shifted(slot, d):
        cur = ext_ref[slot, HALO_ROWS:HALO_ROWS + tm, :]
        if d == 0:
            return cur
        if d < 0:
            blend = jnp.where(sub < HALO_ROWS + d, cur, ext_ref[slot, 0:tm, :])
            rot = -d
        else:
            blend = jnp.where(sub >= d, cur, ext_ref[slot, 2 * HALO_ROWS:2 * HALO_ROWS + tm, :])
            rot = HALO_ROWS - d
        blend = blend.reshape(tm // HALO_ROWS, HALO_ROWS, cw)
        return pltpu.roll(blend, rot, axis=1).reshape(tm, cw)

    def project_xbc(cc):
        slot = (cc + i) % 2
        ext = jnp.dot(h_ext, wzx_ref[:, D_INNER + cc * cw:D_INNER + (cc + 1) * cw], preferred_element_type=F32)
        ext_ref[slot, 0:HALO_ROWS, :] = jnp.where(first, 0.0, ext[:HALO_ROWS])
        ext_ref[slot, HALO_ROWS:HALO_ROWS + tm, :] = ext[HALO_ROWS:HALO_ROWS + tm]
        ext_ref[slot, HALO_ROWS + tm:, :] = jnp.where(last, 0.0, ext[HALO_ROWS + tm:])

    def conv_silu(cc):
        sl = slice(cc * cw, (cc + 1) * cw)
        acc = None
        for k in range(CONV_TAPS):
            term = shifted((cc + i) % 2, k - CONV_TAPS // 2) * cw_ref[k:k + 1, sl]
            acc = term if acc is None else acc + term
        acc = acc + cb_ref[:, sl]
        y = acc * _sigmoid(acc)
        if cc * cw < D_INNER:
            xs_ref[:, sl] = y
        else:
            bc_ref[:, cc * cw - D_INNER:(cc + 1) * cw - D_INNER] = y.astype(BF16)

    step = 256

    def z_part(j):
        zg_ref[:, j * step:(j + 1) * step] = mm(wzx_ref, j * step, (j + 1) * step)

    def gate_part(j):
        zg_ref[:, D_INNER + j * step:D_INNER + (j + 1) * step] = mm(
            wkvg_ref, 2 * KV_WIDTH + j * step, 2 * KV_WIDTH + (j + 1) * step)

    def q_part(j):
        qkv_ref[:, j * step:(j + 1) * step] = mm(wq_ref, j * step, (j + 1) * step).astype(BF16)

    def kv_part(j):
        qkv_ref[:, ATTN_WIDTH + j * step:ATTN_WIDTH + (j + 1) * step] = mm(
            wkvg_ref, j * step, (j + 1) * step).astype(BF16)

    def dt_part():
        dt_ref[...] = mm(wdt_ref, 0, DT_WIDTH)

    plain = ([functools.partial(z_part, j) for j in range(D_INNER // step)]
             + [functools.partial(gate_part, j) for j in range(2 * D_MODEL // step)]
             + [functools.partial(q_part, j) for j in range(ATTN_WIDTH // step)]
             + [functools.partial(kv_part, j) for j in range(2 * KV_WIDTH // step)] + [dt_part])
    n_chunks = CONV_DIM // cw
    project_xbc(0)
    for cc in range(n_chunks):
        if cc + 1 < n_chunks:
            project_xbc(cc + 1)
        conv_silu(cc)
        for job in plain[cc * len(plain) // n_chunks:(cc + 1) * len(plain) // n_chunks]:
            job()


def _inproj(x, g, wzx, wq, wkvg, wdt, conv_w, conv_b, seq, tm=256):
    T = x.shape[0]
    assert seq % tm == 0
    hb = tm // HALO_ROWS
    last_halo = T // HALO_ROWS - 1
    row = lambda i: (i, 0)
    cw = 256
    kern = functools.partial(_inproj_kernel, steps_per_seq=seq // tm, tm=tm, cw=cw)
    return pl.pallas_call(
        kern,
        grid=(T // tm,),
        in_specs=[pl.BlockSpec((HALO_ROWS, D_MODEL), lambda i: (jnp.maximum(i * hb - 1, 0), 0)),
                  pl.BlockSpec((tm, D_MODEL), row),
                  pl.BlockSpec((HALO_ROWS, D_MODEL), lambda i: (jnp.minimum((i + 1) * hb, last_halo), 0)),
                  _resident((1, D_MODEL)), _resident(wzx.shape), _resident(wq.shape), _resident(wkvg.shape),
                  _resident(wdt.shape), _resident((CONV_TAPS, CONV_DIM)), _resident((1, CONV_DIM))],
        out_specs=[pl.BlockSpec((tm, D_INNER), row), pl.BlockSpec((tm, CONV_DIM - D_INNER), row),
                   pl.BlockSpec((tm, ZG_WIDTH), row), pl.BlockSpec((tm, QKV_WIDTH), row),
                   pl.BlockSpec((tm, DT_WIDTH), row)],
        out_shape=[jax.ShapeDtypeStruct((T, D_INNER), F32), jax.ShapeDtypeStruct((T, CONV_DIM - D_INNER), BF16),
                   jax.ShapeDtypeStruct((T, ZG_WIDTH), F32), jax.ShapeDtypeStruct((T, QKV_WIDTH), BF16),
                   jax.ShapeDtypeStruct((T, DT_WIDTH), F32)],
        scratch_shapes=[pltpu.VMEM((2, tm + 2 * HALO_ROWS, cw), F32)],
        compiler_params=_params(),
        name="inproj_conv",
    )(x, x, x, g, wzx, wq, wkvg, wdt, conv_w, conv_b)


def _split3(v):
    hi = v.astype(BF16)
    r1 = v - hi.astype(F32)
    mid = r1.astype(BF16)
    lo = (r1 - mid.astype(F32)).astype(BF16)
    return jnp.concatenate([hi, mid, lo], axis=1)


def _tri_cumsum(adt, upper):
    row = lax.broadcasted_iota(jnp.int32, (CHUNK, CHUNK), 0)
    col = lax.broadcasted_iota(jnp.int32, (CHUNK, CHUNK), 1)
    tri = jnp.where((row <= col) if upper else (row >= col), 1.0, 0.0).astype(BF16)
    p = jnp.dot(tri, _split3(adt), preferred_element_type=F32)
    return p[:, :LANES] + p[:, LANES:2 * LANES] + p[:, 2 * LANES:]


def _pack3(v, lane):
    hi = v.astype(BF16).astype(F32)
    r1 = v - hi
    mid = r1.astype(BF16).astype(F32)
    lo = r1 - mid
    packed = jnp.where(lane < SSD_HEADS, hi,
                       jnp.where(lane < 2 * SSD_HEADS, mid, jnp.where(lane < BWD_LANE, lo, 0.0)))
    return packed.astype(BF16)


def _expand_heads(decay, weight, lane, e_ref):
    packed = jnp.concatenate([_pack3(decay, lane), _pack3(weight, lane)], axis=0)
    ex = jnp.dot(packed, e_ref[...], preferred_element_type=F32)
    return ex[:CHUNK], ex[CHUNK:]


def _ssd_fwd_kernel(xs_ref, bc_ref, dt_ref, dtb_ref, alog_ref, e_ref, y_ref, s_ref, *, steps_per_seq):
    c = pl.program_id(0)

    @pl.when(c % steps_per_seq == 0)
    def _():
        s_ref[...] = jnp.zeros_like(s_ref)

    row = lax.broadcasted_iota(jnp.int32, (CHUNK, CHUNK), 0)
    col = lax.broadcasted_iota(jnp.int32, (CHUNK, CHUNK), 1)
    strict_lower = row > col
    diag = row == col
    lane = lax.broadcasted_iota(jnp.int32, (CHUNK, LANES), 1)
    head_of_lane = lax.broadcasted_iota(jnp.int32, (CHUNK, GROUP_WIDTH), 1) // SSD_HEAD_DIM
    a = -jnp.exp(alog_ref[...])
    nb = SSD_GROUPS * SSD_STATE
    for ci in range(SSD_STEP_CHUNKS):
        rows = slice(ci * CHUNK, (ci + 1) * CHUNK)
        dt = _softplus(dt_ref[rows, :] + dtb_ref[...])
        adt = dt * a
        acs = jnp.where(lane < BWD_LANE, _tri_cumsum(adt, False), _tri_cumsum(adt, True))
        q_t = (acs - jnp.log(dt)).T
        dt_t = dt.T
        log_dt_sum_t = jnp.log(dt_t[0:SSD_HEADS] + dt_t[BWD_LANE:BWD_LANE + SSD_HEADS])
        decay_in, w_state = _expand_heads(jnp.exp(acs), jnp.exp(acs[CHUNK - 1:CHUNK, :] - acs) * dt, lane, e_ref)
        xs = xs_ref[rows, :]
        xw = (xs * w_state).astype(BF16)
        xb = xs.astype(BF16)
        for g in range(SSD_GROUPS):
            gs = slice(g * GROUP_WIDTH, (g + 1) * GROUP_WIDTH)
            b_g = bc_ref[rows, g * SSD_STATE:(g + 1) * SSD_STATE]
            c_g = bc_ref[rows, nb + g * SSD_STATE:nb + (g + 1) * SSD_STATE]
            cb = lax.dot_general(c_g, b_g, _NT, preferred_element_type=F32)
            x_g = xb[:, gs]
            ms, xm = [], []
            for r in range(HEADS_PER_GROUP):
                hf = g * HEADS_PER_GROUP + r
                hb = BWD_LANE + hf
                arg = jnp.where(strict_lower, acs[:, hf:hf + 1] - q_t[hf:hf + 1, :],
                                jnp.where(diag, log_dt_sum_t[hf:hf + 1, :], acs[:, hb:hb + 1] - q_t[hb:hb + 1, :]))
                ms.append((cb * jnp.exp(arg)).astype(BF16))
                xm.append(jnp.where(head_of_lane == r, x_g, jnp.zeros_like(x_g)))
            y_diag = (jnp.dot(jnp.concatenate(ms[0:2], axis=1), jnp.concatenate(xm[0:2], axis=0),
                              preferred_element_type=F32)
                      + jnp.dot(jnp.concatenate(ms[2:4], axis=1), jnp.concatenate(xm[2:4], axis=0),
                                preferred_element_type=F32))
            s_g = s_ref[g]
            y_off = jnp.dot(c_g, s_g.astype(BF16), preferred_element_type=F32) * decay_in[:, gs]
            y_ref[rows, gs] = y_diag + y_off
            upd = lax.dot_general(b_g, xw[:, gs], _TN, preferred_element_type=F32)
            s_ref[g] = s_g * decay_in[CHUNK - 1:CHUNK, gs] + upd


def _ssd_fwd(xs, bc, dt, dtb, alog, e, seq):
    T = xs.shape[0]
    rows = SSD_STEP_CHUNKS * CHUNK
    assert seq % rows == 0
    row = lambda c: (c, 0)
    kern = functools.partial(_ssd_fwd_kernel, steps_per_seq=seq // rows)
    return pl.pallas_call(
        kern,
        grid=(T // rows,),
        in_specs=[pl.BlockSpec((rows, D_INNER), row), pl.BlockSpec((rows, CONV_DIM - D_INNER), row),
                  pl.BlockSpec((rows, LANES), row), pl.BlockSpec((1, LANES), lambda c: (0, 0)),
                  pl.BlockSpec((1, LANES), lambda c: (0, 0)), _resident(e.shape)],
        out_specs=pl.BlockSpec((rows, D_INNER), row),
        out_shape=jax.ShapeDtypeStruct((T, D_INNER), F32),
        scratch_shapes=[pltpu.VMEM((SSD_GROUPS, SSD_STATE, GROUP_WIDTH), F32)],
        compiler_params=_params(),
        name="ssd_fwd",
    )(xs, bc, dt, dtb, alog, e)


def _ssd_bwd_kernel(xs_ref, bc_ref, dt_ref, dtb_ref, alog_ref, e_ref, yp_ref, z_ref, gs_ref, ga_ref, x_ref,
                    dskip_ref, nw_ref, wso_ref, bg_ref, wo_ref, pmn_ref, out_ref, s_ref, yn_ref, *,
                    steps_per_seq):
    c = pl.program_id(0)

    @pl.when(c % steps_per_seq == 0)
    def _():
        s_ref[...] = jnp.zeros_like(s_ref)

    lane = lax.broadcasted_iota(jnp.int32, (CHUNK, LANES), 1)
    a = -jnp.exp(alog_ref[...])
    nb = SSD_GROUPS * SSD_STATE
    for ci in reversed(range(SSD_STEP_CHUNKS)):
        rows = slice(ci * CHUNK, (ci + 1) * CHUNK)
        dt = _softplus(dt_ref[rows, :] + dtb_ref[...])
        acs = _tri_cumsum(dt * a, True)
        decay_in, w_state = _expand_heads(jnp.exp(acs), jnp.exp(acs[0:1, :] - acs) * dt, lane, e_ref)
        xs = xs_ref[rows, :]
        xw = (xs * w_state).astype(BF16)
        for g in range(SSD_GROUPS):
            gs = slice(g * GROUP_WIDTH, (g + 1) * GROUP_WIDTH)
            b_g = bc_ref[rows, g * SSD_STATE:(g + 1) * SSD_STATE]
            c_g = bc_ref[rows, nb + g * SSD_STATE:nb + (g + 1) * SSD_STATE]
            s_g = s_ref[g]
            y_off = jnp.dot(c_g, s_g.astype(BF16), preferred_element_type=F32) * decay_in[:, gs]
            y = yp_ref[rows, gs] + y_off + xs[:, gs] * dskip_ref[:, gs]
            z = z_ref[rows, gs]
            u = y * (z * _sigmoid(z))
            yn_ref[rows, gs] = _rms(u, nw_ref[:, gs]).astype(BF16)
            upd = lax.dot_general(b_g, xw[:, gs], _TN, preferred_element_type=F32)
            s_ref[g] = s_g * decay_in[0:1, gs] + upd
    y_ssd = jnp.dot(yn_ref[...], wso_ref[...], preferred_element_type=F32)
    gate = _sigmoid(gs_ref[...] + bg_ref[...])
    mix = (gate * y_ssd + ga_ref[...]).astype(BF16)
    mixed = jnp.dot(mix, wo_ref[...], preferred_element_type=F32)
    out_ref[...] = x_ref[...] + _rms(mixed, pmn_ref[...])


def _ssd_bwd(xs, bc, dt, dtb, alog, e, yp, zg, ga, x, dskip, nw, wso, bg, wo, pmn, seq):
    T = xs.shape[0]
    rows = SSD_STEP_CHUNKS * CHUNK
    last = T // rows - 1
    rev = lambda c: (last - c, 0)
    kern = functools.partial(_ssd_bwd_kernel, steps_per_seq=seq // rows)
    return pl.pallas_call(
        kern,
        grid=(T // rows,),
        in_specs=[pl.BlockSpec((rows, D_INNER), rev), pl.BlockSpec((rows, CONV_DIM - D_INNER), rev),
                  pl.BlockSpec((rows, LANES), lambda c: (last - c, 1)),
                  pl.BlockSpec((1, LANES), lambda c: (0, 1)), pl.BlockSpec((1, LANES), lambda c: (0, 1)),
                  _resident(e.shape),
                  pl.BlockSpec((rows, D_INNER), rev),
                  pl.BlockSpec((rows, D_INNER), rev),
                  pl.BlockSpec((rows, D_MODEL), lambda c: (last - c, 2)),
                  pl.BlockSpec((rows, D_MODEL), rev), pl.BlockSpec((rows, D_MODEL), rev),
                  _resident((1, D_INNER)), _resident((1, D_INNER)), _resident((D_INNER, D_MODEL)),
                  _resident((1, D_MODEL)), _resident((D_MODEL, D_MODEL)), _resident((1, D_MODEL))],
        out_specs=pl.BlockSpec((rows, D_MODEL), rev),
        out_shape=jax.ShapeDtypeStruct((T, D_MODEL), F32),
        scratch_shapes=[pltpu.VMEM((SSD_GROUPS, SSD_STATE, GROUP_WIDTH), F32), pltpu.VMEM((rows, D_INNER), BF16)],
        compiler_params=_params(),
        name="ssd_bwd_mix",
    )(xs, bc, dt, dtb, alog, e, yp, zg, zg, ga, x, dskip, nw, wso, bg, wo, pmn)


def _attn_kernel(q_ref, kp_ref, kc_ref, kn_ref, vp_ref, vc_ref, vn_ref, bias_ref, sink_ref, gate_ref, bg_ref,
                 wao_ref, out_ref, *, nb):
    n = pl.program_id(0)
    k_all = jnp.concatenate([kp_ref[...], kc_ref[...], kn_ref[...]], axis=0)
    v_all = jnp.concatenate([vp_ref[...], vc_ref[...], vn_ref[...]], axis=0)
    low_half = lax.broadcasted_iota(jnp.int32, (ATTN_BLOCK, LANES), 1) < HEAD_DIM
    n_tiles = KV_HEADS // 2
    rep = Q_HEADS // KV_HEADS
    half_rows = rep * ATTN_BLOCK

    def head_group(j, t, e):
        blk = n * ATTN_STEP_BLOCKS + j
        variant = ((blk % nb) == 0).astype(jnp.int32) + 2 * ((blk % nb) == nb - 1).astype(jnp.int32)
        kt = k_all[j * ATTN_BLOCK:(j + 3) * ATTN_BLOCK, t * LANES:(t + 1) * LANES]
        vt = v_all[j * ATTN_BLOCK:(j + 3) * ATTN_BLOCK, t * LANES:(t + 1) * LANES]
        keep = low_half if e == 0 else jnp.logical_not(low_half)
        qs = []
        for r in range(rep):
            qt = q_ref[j * ATTN_BLOCK:(j + 1) * ATTN_BLOCK, (rep * t + r) * LANES:(rep * t + r + 1) * LANES]
            qs.append(jnp.where(keep, qt, jnp.zeros_like(qt)))
        qz = jnp.concatenate(qs, axis=0)
        hrows = slice(e * half_rows, (e + 1) * half_rows)
        s = lax.dot_general(qz, kt, _NT, preferred_element_type=F32) + bias_ref[variant, t, hrows, :]
        sk = sink_ref[t, hrows, :]
        m = jnp.maximum(jnp.max(s, axis=-1, keepdims=True), sk)
        p = jnp.exp(s - m).astype(BF16)
        o = jnp.dot(p, jnp.concatenate([vt, jnp.ones_like(vt)], axis=1), preferred_element_type=F32)
        return o[:, :LANES] * (1.0 / (o[:, LANES:] + jnp.exp(sk - m)))

    block_outs = []
    for j in range(ATTN_STEP_BLOCKS):
        tiles = []
        for t in range(n_tiles):
            lo, hi = head_group(j, t, 0), head_group(j, t, 1)
            for r in range(rep):
                rr = slice(r * ATTN_BLOCK, (r + 1) * ATTN_BLOCK)
                tiles.append(jnp.where(low_half, lo[rr], hi[rr]).astype(BF16))
        block_outs.append(jnp.concatenate(tiles, axis=1))
    o = jnp.concatenate(block_outs, axis=0)
    y = jnp.dot(o, wao_ref[...], preferred_element_type=F32)
    out_ref[...] = _sigmoid(gate_ref[...] + bg_ref[...]) * y


def _attn(qkv, bias, sink, zg, bg, wao, seq):
    T = qkv.shape[0]
    nb = seq // ATTN_BLOCK
    nblk = T // ATTN_BLOCK
    rows = ATTN_STEP_BLOCKS * ATTN_BLOCK
    assert seq % rows == 0
    kcol = ATTN_WIDTH // KV_WIDTH
    vcol = kcol + 1
    prev = lambda n: jnp.maximum(n * ATTN_STEP_BLOCKS - 1, 0)
    nxt = lambda n: jnp.minimum((n + 1) * ATTN_STEP_BLOCKS, nblk - 1)
    kern = functools.partial(_attn_kernel, nb=nb)
    return pl.pallas_call(
        kern,
        grid=(T // rows,),
        in_specs=[pl.BlockSpec((rows, ATTN_WIDTH), lambda n: (n, 0)),
                  pl.BlockSpec((ATTN_BLOCK, KV_WIDTH), lambda n: (prev(n), kcol)),
                  pl.BlockSpec((rows, KV_WIDTH), lambda n: (n, kcol)),
                  pl.BlockSpec((ATTN_BLOCK, KV_WIDTH), lambda n: (nxt(n), kcol)),
                  pl.BlockSpec((ATTN_BLOCK, KV_WIDTH), lambda n: (prev(n), vcol)),
                  pl.BlockSpec((rows, KV_WIDTH), lambda n: (n, vcol)),
                  pl.BlockSpec((ATTN_BLOCK, KV_WIDTH), lambda n: (nxt(n), vcol)),
                  _resident(bias.shape), _resident(sink.shape),
                  pl.BlockSpec((rows, D_MODEL), lambda n: (n, 3)),
                  _resident((1, D_MODEL)), _resident((ATTN_WIDTH, D_MODEL))],
        out_specs=pl.BlockSpec((rows, D_MODEL), lambda n: (n, 0)),
        out_shape=jax.ShapeDtypeStruct((T, D_MODEL), F32),
        compiler_params=_params(),
        name="attn",
    )(qkv, qkv, qkv, qkv, qkv, qkv, qkv, bias, sink, zg, bg, wao)


def _mlp_kernel(x_ref, g1_ref, w1_ref, w2_ref, g2_ref, out_ref):
    x = x_ref[...]
    h = _rms(x, g1_ref[...]).astype(BF16)
    step = 1024
    acc = None
    for j in range(D_FF // step):
        a = jnp.dot(h, w1_ref[:, j * step:(j + 1) * step], preferred_element_type=F32)
        a = jnp.square(jnp.maximum(a, 0.0)).astype(BF16)
        t = jnp.dot(a, w2_ref[j * step:(j + 1) * step, :], preferred_element_type=F32)
        acc = t if acc is None else acc + t
    out_ref[...] = x + _rms(acc, g2_ref[...])


def _mlp(x, g1, w1, w2, g2, tm=256):
    T = x.shape[0]
    row = lambda i: (i, 0)
    return pl.pallas_call(
        _mlp_kernel,
        grid=(T // tm,),
        in_specs=[pl.BlockSpec((tm, D_MODEL), row), _resident((1, D_MODEL)), _resident((D_MODEL, D_FF)),
                  _resident((D_FF, D_MODEL)), _resident((1, D_MODEL))],
        out_specs=pl.BlockSpec((tm, D_MODEL), row),
        out_shape=jax.ShapeDtypeStruct((T, D_MODEL), F32),
        compiler_params=_params(),
        name="mlp",
    )(x, g1, w1, w2, g2)


def _permute_heads(a, axis):
    shape = a.shape
    a = a.reshape(shape[:axis] + (KV_HEADS // 2, 2, 4) + shape[axis + 1:])
    a = jnp.swapaxes(a, axis + 1, axis + 2)
    return a.reshape(shape)


def _dt_layout(fwd, bwd, axis=-1):
    return jnp.concatenate([fwd, fwd, fwd, bwd, bwd, bwd, bwd, jnp.zeros_like(bwd)], axis=axis)


def _transpose_cast_kernel(w_ref, o_ref):
    o_ref[...] = w_ref[...].T.astype(BF16)


def _transpose_cast(wt, rows=256):
    n, k = wt.shape
    assert n % rows == 0
    return pl.pallas_call(
        _transpose_cast_kernel,
        grid=(n // rows,),
        in_specs=[pl.BlockSpec((rows, k), lambda i: (i, 0))],
        out_specs=pl.BlockSpec((k, rows), lambda i: (0, i)),
        out_shape=jax.ShapeDtypeStruct((k, n), BF16),
        compiler_params=_params(),
        name="weight_transpose",
    )(wt)


def _prep_w_in(w):
    wt = w.T
    dt0 = D_INNER + CONV_DIM
    q0 = dt0 + 2 * SSD_HEADS
    k0 = q0 + ATTN_WIDTH
    wzx = _transpose_cast(wt[:dt0])
    wq = wt[q0:k0].reshape(Q_HEADS, HEAD_DIM, D_MODEL) * (HEAD_DIM ** -0.5)
    wq = _permute_heads(wq, 0).reshape(ATTN_WIDTH, D_MODEL).T.astype(BF16)
    wkvg = _transpose_cast(wt[k0:])
    wdt = _dt_layout(wt[dt0:dt0 + SSD_HEADS], wt[dt0 + SSD_HEADS:q0], axis=0).T.astype(BF16)
    return wzx, wq, wkvg, wdt


def _expand_matrix():
    e = np.zeros((LANES, D_INNER), np.float32)
    for p in range(3):
        for h in range(SSD_HEADS):
            e[p * SSD_HEADS + h, h * SSD_HEAD_DIM:(h + 1) * SSD_HEAD_DIM] = 1.0
    return jnp.asarray(e, BF16)


def _t5_bucket(rel):
    nb = N_BUCKETS // 2
    max_exact = nb // 2
    ret = jnp.where(rel > 0, nb, 0)
    n = jnp.abs(rel)
    nf = jnp.maximum(n, 1).astype(F32)
    large = max_exact + (jnp.log(nf / max_exact) / math.log(MAX_DISTANCE / max_exact)
                         * (nb - max_exact)).astype(jnp.int32)
    large = jnp.minimum(large, nb - 1)
    return ret + jnp.where(n < max_exact, n, large)


def _window_bias(rel_table):
    period = 4 * ATTN_BLOCK
    d = jnp.arange(period)
    d = jnp.where(d >= 3 * ATTN_BLOCK, d - period, d)
    rel = d - ATTN_BLOCK
    vec = rel_table[_t5_bucket(rel)].astype(F32)
    vec = jnp.where((jnp.abs(rel) <= WINDOW)[:, None], vec, -jnp.inf).T
    skew = jnp.tile(vec, (1, ATTN_BLOCK))[:, :ATTN_BLOCK * (period - 1)]
    bias = skew.reshape(Q_HEADS, ATTN_BLOCK, period - 1)[:, :, :3 * ATTN_BLOCK]
    bias = bias.reshape(KV_HEADS // 2, 8 * ATTN_BLOCK, 3 * ATTN_BLOCK)
    col = jnp.arange(3 * ATTN_BLOCK)
    no_prev = jnp.where(col < ATTN_BLOCK, -jnp.inf, 0.0).astype(F32)
    no_next = jnp.where(col >= 2 * ATTN_BLOCK, -jnp.inf, 0.0).astype(F32)
    return jnp.stack([bias, bias + no_prev, bias + no_next, bias + no_prev + no_next])


def _stack_sink(sink):
    return jnp.repeat(sink.astype(F32), ATTN_BLOCK).reshape(KV_HEADS // 2, 8 * ATTN_BLOCK, 1)


def kernel(x, pre_mix_norm, w_in, b_gate, conv_w, conv_b, dt_bias, a_log, d_skip, ssd_norm, w_ssd_out, attn_sink,
           rel_bias_table, w_attn_out, w_o, post_mix_norm, pre_mlp_norm, w_mlp_in, w_mlp_out, post_mlp_norm):
    bsz, seq, d = x.shape
    assert d == D_MODEL and seq % (SSD_STEP_CHUNKS * CHUNK) == 0 and seq % ATTN_BLOCK == 0
    depth = w_in.shape[0]
    xf = x.reshape(bsz * seq, d)
    e = _expand_matrix()
    bias = _window_bias(rel_bias_table)
    r1 = lambda v: v.reshape(1, -1)
    for l in range(depth):
        xs, bc, zg, qkv, dtr = _inproj(xf, r1(pre_mix_norm[l]), *_prep_w_in(w_in[l]),
                                       conv_w[l].reshape(CONV_TAPS, CONV_DIM), r1(conv_b[l]), seq)
        dtb = r1(_dt_layout(dt_bias[l, 0], dt_bias[l, 1]).astype(F32))
        alog = r1(_dt_layout(a_log[l, 0], a_log[l, 1]).astype(F32))
        yp = _ssd_fwd(xs, bc, dtr, dtb, alog, e, seq)
        wao = _permute_heads(w_attn_out[l].reshape(Q_HEADS, HEAD_DIM, D_MODEL), 0).reshape(ATTN_WIDTH, D_MODEL)
        ga = _attn(qkv, bias, _stack_sink(attn_sink[l]), zg, r1(b_gate[l, D_MODEL:]), wao.astype(BF16), seq)
        dskip = r1(jnp.repeat(d_skip[l], SSD_HEAD_DIM))
        xf = _ssd_bwd(xs, bc, dtr, dtb, alog, e, yp, zg, ga, xf, dskip, r1(ssd_norm[l]),
                      w_ssd_out[l].astype(BF16), r1(b_gate[l, :D_MODEL]), w_o[l].astype(BF16),
                      r1(post_mix_norm[l]), seq)
        xf = _mlp(xf, r1(pre_mlp_norm[l]), w_mlp_in[l].astype(BF16), w_mlp_out[l].astype(BF16),
                  r1(post_mlp_norm[l]))
    return xf.reshape(bsz, seq, d)
```

```python
import functools
import math

import numpy as np
import jax
import jax.numpy as jnp
from jax import lax
from jax.experimental import pallas as pl
from jax.experimental.pallas import tpu as pltpu

F32 = jnp.float32
BF16 = jnp.bfloat16
EPS = 1e-6

D_MODEL = 1024
D_INNER = 2048
SSD_HEADS = 32
SSD_HEAD_DIM = 64
SSD_GROUPS = 8
HEADS_PER_GROUP = SSD_HEADS // SSD_GROUPS
GROUP_WIDTH = D_INNER // SSD_GROUPS
SSD_STATE = 128
CHUNK = 128
CONV_TAPS = 5
CONV_DIM = D_INNER + 2 * SSD_GROUPS * SSD_STATE
Q_HEADS = 16
KV_HEADS = 4
HEAD_DIM = 64
ATTN_WIDTH = Q_HEADS * HEAD_DIM
KV_WIDTH = KV_HEADS * HEAD_DIM
WINDOW = 128
ATTN_BLOCK = 128
N_BUCKETS = 32
MAX_DISTANCE = 128
D_FF = 4 * D_MODEL
LANES = 128
HALO_ROWS = 8
DT_WIDTH = 2 * LANES
BWD_LANE = 3 * SSD_HEADS

ZG_WIDTH = D_INNER + 2 * D_MODEL
KVG_OFF = D_INNER + CONV_DIM
QKV_WIDTH = ATTN_WIDTH + 2 * KV_WIDTH
SSD_STEP_CHUNKS = 2
ATTN_STEP_BLOCKS = 2

VMEM_LIMIT = 56 * 1024 * 1024


def _params(n_axes=1, flags=None):
    return pltpu.CompilerParams(dimension_semantics=("arbitrary",) * n_axes,
                                vmem_limit_bytes=VMEM_LIMIT, flags=flags)


def _resident(shape):
    nd = len(shape)
    return pl.BlockSpec(shape, lambda *_: (0,) * nd, pipeline_mode=pl.Buffered(1))


_NT = (((1,), (1,)), ((), ()))
_TN = (((0,), (0,)), ((), ()))


def _rms(x, g):
    ms = jnp.mean(x * x, axis=-1, keepdims=True)
    return x * lax.rsqrt(ms + EPS) * g


def _sigmoid(x):
    return 1.0 / (1.0 + jnp.exp(-x))


def _softplus(x):
    e = jnp.exp(-jnp.abs(x))
    u = 1.0 + e
    log1p_e = jnp.where(u == 1.0, e, jnp.log(u) * (e / (u - 1.0)))
    return jnp.maximum(x, 0.0) + log1p_e


def _inproj_kernel(xp_ref, x_ref, xn_ref, g_ref, wbig_ref, wq_ref, wdt_ref, cw_ref, cb_ref,
                   xs_ref, bc_ref, zg_ref, qkv_ref, dt_ref, ext_ref, *, steps_per_seq, tm, cw):
    i = pl.program_id(0)
    first = (i % steps_per_seq) == 0
    last = (i % steps_per_seq) == steps_per_seq - 1
    g = g_ref[...]
    x = x_ref[...]
    h = _rms(x, g).astype(BF16)
    h_ext = _rms(jnp.concatenate([xp_ref[...], x, xn_ref[...]], axis=0), g).astype(BF16)

    def mm(w_ref, lo, hi):
        return jnp.dot(h, w_ref[:, lo:hi], preferred_element_type=F32)

    sub = lax.broadcasted_iota(jnp.int32, (tm, cw), 0) % HALO_ROWS

    def shifted(slot, d):
        cur = ext_ref[slot, HALO_ROWS:HALO_ROWS + tm, :]
        if d == 0:
            return cur
        if d < 0:
            blend = jnp.where(sub < HALO_ROWS + d, cur, ext_ref[slot, 0:tm, :])
            rot = -d
        else:
            blend = jnp.where(sub >= d, cur, ext_ref[slot, 2 * HALO_ROWS:2 * HALO_ROWS + tm, :])
            rot = HALO_ROWS - d
        blend = blend.reshape(tm // HALO_ROWS, HALO_ROWS, cw)
        return pltpu.roll(blend, rot, axis=1).reshape(tm, cw)

    def project_xbc(cc):
        slot = (cc + i) % 2
        ext = jnp.dot(h_ext, wbig_ref[:, D_INNER + cc * cw:D_INNER + (cc + 1) * cw], preferred_element_type=F32)
        ext_ref[slot, 0:HALO_ROWS, :] = jnp.where(first, 0.0, ext[:HALO_ROWS])
        ext_ref[slot, HALO_ROWS:HALO_ROWS + tm, :] = ext[HALO_ROWS:HALO_ROWS + tm]
        ext_ref[slot, HALO_ROWS + tm:, :] = jnp.where(last, 0.0, ext[HALO_ROWS + tm:])

    def conv_silu(cc):
        sl = slice(cc * cw, (cc + 1) * cw)
        acc = None
        for k in range(CONV_TAPS):
            term = shifted((cc + i) % 2, k - CONV_TAPS // 2) * cw_ref[k:k + 1, sl]
            acc = term if acc is None else acc + term
        acc = acc + cb_ref[:, sl]
        y = acc * _sigmoid(acc)
        if cc * cw < D_INNER:
            xs_ref[:, sl] = y
        else:
            bc_ref[:, cc * cw - D_INNER:(cc + 1) * cw - D_INNER] = y.astype(BF16)

    step = 256

    def z_part(j):
        zg_ref[:, j * step:(j + 1) * step] = mm(wbig_ref, j * step, (j + 1) * step)

    def gate_part(j):
        zg_ref[:, D_INNER + j * step:D_INNER + (j + 1) * step] = mm(
            wbig_ref, KVG_OFF + 2 * KV_WIDTH + j * step, KVG_OFF + 2 * KV_WIDTH + (j + 1) * step)

    def q_part(j):
        qkv_ref[:, j * step:(j + 1) * step] = mm(wq_ref, j * step, (j + 1) * step).astype(BF16)

    def kv_part(j):
        qkv_ref[:, ATTN_WIDTH + j * step:ATTN_WIDTH + (j + 1) * step] = mm(
            wbig_ref, KVG_OFF + j * step, KVG_OFF + (j + 1) * step).astype(BF16)

    def dt_part():
        dt_ref[...] = mm(wdt_ref, 0, DT_WIDTH)

    plain = ([functools.partial(z_part, j) for j in range(D_INNER // step)]
             + [functools.partial(gate_part, j) for j in range(2 * D_MODEL // step)]
             + [functools.partial(q_part, j) for j in range(ATTN_WIDTH // step)]
             + [functools.partial(kv_part, j) for j in range(2 * KV_WIDTH // step)] + [dt_part])
    n_chunks = CONV_DIM // cw
    project_xbc(0)
    for cc in range(n_chunks):
        if cc + 1 < n_chunks:
            project_xbc(cc + 1)
        conv_silu(cc)
        for job in plain[cc * len(plain) // n_chunks:(cc + 1) * len(plain) // n_chunks]:
            job()


def _inproj(x, g, wbig, layer, wq, wdt, conv_w, conv_b, seq, tm=256):
    T = x.shape[0]
    assert seq % tm == 0
    hb = tm // HALO_ROWS
    last_halo = T // HALO_ROWS - 1
    row = lambda i: (i, 0)
    cw = 256
    kern = functools.partial(_inproj_kernel, steps_per_seq=seq // tm, tm=tm, cw=cw)
    return pl.pallas_call(
        kern,
        grid=(T // tm,),
        in_specs=[pl.BlockSpec((HALO_ROWS, D_MODEL), lambda i: (jnp.maximum(i * hb - 1, 0), 0)),
                  pl.BlockSpec((tm, D_MODEL), row),
                  pl.BlockSpec((HALO_ROWS, D_MODEL), lambda i: (jnp.minimum((i + 1) * hb, last_halo), 0)),
                  _resident((1, D_MODEL)),
                  pl.BlockSpec((None,) + wbig.shape[1:], lambda i: (layer, 0, 0), pipeline_mode=pl.Buffered(1)),
                  _resident(wq.shape), _resident(wdt.shape), _resident((CONV_TAPS, CONV_DIM)),
                  _resident((1, CONV_DIM))],
        out_specs=[pl.BlockSpec((tm, D_INNER), row), pl.BlockSpec((tm, CONV_DIM - D_INNER), row),
                   pl.BlockSpec((tm, ZG_WIDTH), row), pl.BlockSpec((tm, QKV_WIDTH), row),
                   pl.BlockSpec((tm, DT_WIDTH), row)],
        out_shape=[jax.ShapeDtypeStruct((T, D_INNER), F32), jax.ShapeDtypeStruct((T, CONV_DIM - D_INNER), BF16),
                   jax.ShapeDtypeStruct((T, ZG_WIDTH), F32), jax.ShapeDtypeStruct((T, QKV_WIDTH), BF16),
                   jax.ShapeDtypeStruct((T, DT_WIDTH), F32)],
        scratch_shapes=[pltpu.VMEM((2, tm + 2 * HALO_ROWS, cw), F32)],
        compiler_params=_params(),
        name="inproj_conv",
    )(x, x, x, g, wbig, wq, wdt, conv_w, conv_b)


def _split3(v):
    hi = v.astype(BF16)
    r1 = v - hi.astype(F32)
    mid = r1.astype(BF16)
    lo = (r1 - mid.astype(F32)).astype(BF16)
    return jnp.concatenate([hi, mid, lo], axis=1)


def _tri_cumsum(adt, upper):
    row = lax.broadcasted_iota(jnp.int32, (CHUNK, CHUNK), 0)
    col = lax.broadcasted_iota(jnp.int32, (CHUNK, CHUNK), 1)
    tri = jnp.where((row <= col) if upper else (row >= col), 1.0, 0.0).astype(BF16)
    p = jnp.dot(tri, _split3(adt), preferred_element_type=F32)
    return p[:, :LANES] + p[:, LANES:2 * LANES] + p[:, 2 * LANES:]


def _pack3(v, lane):
    hi = v.astype(BF16).astype(F32)
    r1 = v - hi
    mid = r1.astype(BF16).astype(F32)
    lo = r1 - mid
    packed = jnp.where(lane < SSD_HEADS, hi,
                       jnp.where(lane < 2 * SSD_HEADS, mid, jnp.where(lane < BWD_LANE, lo, 0.0)))
    return packed.astype(BF16)


def _expand_heads(decay, weight, lane, e_ref):
    packed = jnp.concatenate([_pack3(decay, lane), _pack3(weight, lane)], axis=0)
    ex = jnp.dot(packed, e_ref[...], preferred_element_type=F32)
    return ex[:CHUNK], ex[CHUNK:]


def _ssd_fwd_kernel(xs_ref, bc_ref, dt_ref, dtb_ref, alog_ref, e_ref, y_ref, s_ref, *, steps_per_seq):
    c = pl.program_id(0)

    @pl.when(c % steps_per_seq == 0)
    def _():
        s_ref[...] = jnp.zeros_like(s_ref)

    row = lax.broadcasted_iota(jnp.int32, (CHUNK, CHUNK), 0)
    col = lax.broadcasted_iota(jnp.int32, (CHUNK, CHUNK), 1)
    strict_lower = row > col
    diag = row == col
    lane = lax.broadcasted_iota(jnp.int32, (CHUNK, LANES), 1)
    head_of_lane = lax.broadcasted_iota(jnp.int32, (CHUNK, GROUP_WIDTH), 1) // SSD_HEAD_DIM
    a = -jnp.exp(alog_ref[...])
    nb = SSD_GROUPS * SSD_STATE
    for ci in range(SSD_STEP_CHUNKS):
        rows = slice(ci * CHUNK, (ci + 1) * CHUNK)
        dt = _softplus(dt_ref[rows, :] + dtb_ref[...])
        adt = dt * a
        acs = jnp.where(lane < BWD_LANE, _tri_cumsum(adt, False), _tri_cumsum(adt, True))
        q_t = (acs - jnp.log(dt)).T
        dt_t = dt.T
        log_dt_sum_t = jnp.log(dt_t[0:SSD_HEADS] + dt_t[BWD_LANE:BWD_LANE + SSD_HEADS])
        decay_in, w_state = _expand_heads(jnp.exp(acs), jnp.exp(acs[CHUNK - 1:CHUNK, :] - acs) * dt, lane, e_ref)
        xs = xs_ref[rows, :]
        xw = (xs * w_state).astype(BF16)
        xb = xs.astype(BF16)
        for g in range(SSD_GROUPS):
            gs = slice(g * GROUP_WIDTH, (g + 1) * GROUP_WIDTH)
            b_g = bc_ref[rows, g * SSD_STATE:(g + 1) * SSD_STATE]
            c_g = bc_ref[rows, nb + g * SSD_STATE:nb + (g + 1) * SSD_STATE]
            cb = lax.dot_general(c_g, b_g, _NT, preferred_element_type=F32)
            x_g = xb[:, gs]
            ms, xm = [], []
            for r in range(HEADS_PER_GROUP):
                hf = g * HEADS_PER_GROUP + r
                hb = BWD_LANE + hf
                arg = jnp.where(strict_lower, acs[:, hf:hf + 1] - q_t[hf:hf + 1, :],
                                jnp.where(diag, log_dt_sum_t[hf:hf + 1, :], acs[:, hb:hb + 1] - q_t[hb:hb + 1, :]))
                ms.append((cb * jnp.exp(arg)).astype(BF16))
                xm.append(jnp.where(head_of_lane == r, x_g, jnp.zeros_like(x_g)))
            y_diag = (jnp.dot(jnp.concatenate(ms[0:2], axis=1), jnp.concatenate(xm[0:2], axis=0),
                              preferred_element_type=F32)
                      + jnp.dot(jnp.concatenate(ms[2:4], axis=1), jnp.concatenate(xm[2:4], axis=0),
                                preferred_element_type=F32))
            s_g = s_ref[g]
            y_off = jnp.dot(c_g, s_g.astype(BF16), preferred_element_type=F32) * decay_in[:, gs]
            y_ref[rows, gs] = y_diag + y_off
            upd = lax.dot_general(b_g, xw[:, gs], _TN, preferred_element_type=F32)
            s_ref[g] = s_g * decay_in[CHUNK - 1:CHUNK, gs] + upd


def _ssd_fwd(xs, bc, dt, dtb, alog, e, seq):
    T = xs.shape[0]
    rows = SSD_STEP_CHUNKS * CHUNK
    assert seq % rows == 0
    row = lambda c: (c, 0)
    kern = functools.partial(_ssd_fwd_kernel, steps_per_seq=seq // rows)
    return pl.pallas_call(
        kern,
        grid=(T // rows,),
        in_specs=[pl.BlockSpec((rows, D_INNER), row), pl.BlockSpec((rows, CONV_DIM - D_INNER), row),
                  pl.BlockSpec((rows, LANES), row), pl.BlockSpec((1, LANES), lambda c: (0, 0)),
                  pl.BlockSpec((1, LANES), lambda c: (0, 0)), _resident(e.shape)],
        out_specs=pl.BlockSpec((rows, D_INNER), row),
        out_shape=jax.ShapeDtypeStruct((T, D_INNER), F32),
        scratch_shapes=[pltpu.VMEM((SSD_GROUPS, SSD_STATE, GROUP_WIDTH), F32)],
        compiler_params=_params(),
        name="ssd_fwd",
    )(xs, bc, dt, dtb, alog, e)


def _ssd_bwd_kernel(xs_ref, bc_ref, dt_ref, dtb_ref, alog_ref, e_ref, yp_ref, z_ref, gs_ref, ga_ref, x_ref,
                    dskip_ref, nw_ref, wso_ref, bg_ref, wo_ref, pmn_ref, out_ref, s_ref, yn_ref, *,
                    steps_per_seq):
    c = pl.program_id(0)

    @pl.when(c % steps_per_seq == 0)
    def _():
        s_ref[...] = jnp.zeros_like(s_ref)

    lane = lax.broadcasted_iota(jnp.int32, (CHUNK, LANES), 1)
    a = -jnp.exp(alog_ref[...])
    nb = SSD_GROUPS * SSD_STATE
    for ci in reversed(range(SSD_STEP_CHUNKS)):
        rows = slice(ci * CHUNK, (ci + 1) * CHUNK)
        dt = _softplus(dt_ref[rows, :] + dtb_ref[...])
        acs = _tri_cumsum(dt * a, True)
        decay_in, w_state = _expand_heads(jnp.exp(acs), jnp.exp(acs[0:1, :] - acs) * dt, lane, e_ref)
        xs = xs_ref[rows, :]
        xw = (xs * w_state).astype(BF16)
        for g in range(SSD_GROUPS):
            gs = slice(g * GROUP_WIDTH, (g + 1) * GROUP_WIDTH)
            b_g = bc_ref[rows, g * SSD_STATE:(g + 1) * SSD_STATE]
            c_g = bc_ref[rows, nb + g * SSD_STATE:nb + (g + 1) * SSD_STATE]
            s_g = s_ref[g]
            y_off = jnp.dot(c_g, s_g.astype(BF16), preferred_element_type=F32) * decay_in[:, gs]
            y = yp_ref[rows, gs] + y_off + xs[:, gs] * dskip_ref[:, gs]
            z = z_ref[rows, gs]
            u = y * (z * _sigmoid(z))
            yn_ref[rows, gs] = _rms(u, nw_ref[:, gs]).astype(BF16)
            upd = lax.dot_general(b_g, xw[:, gs], _TN, preferred_element_type=F32)
            s_ref[g] = s_g * decay_in[0:1, gs] + upd
    y_ssd = jnp.dot(yn_ref[...], wso_ref[...], preferred_element_type=F32)
    gate = _sigmoid(gs_ref[...] + bg_ref[...])
    mix = (gate * y_ssd + ga_ref[...]).astype(BF16)
    mixed = jnp.dot(mix, wo_ref[...], preferred_element_type=F32)
    out_ref[...] = x_ref[...] + _rms(mixed, pmn_ref[...])


def _ssd_bwd(xs, bc, dt, dtb, alog, e, yp, zg, ga, x, dskip, nw, wso, bg, wo, pmn, seq):
    T = xs.shape[0]
    rows = SSD_STEP_CHUNKS * CHUNK
    last = T // rows - 1
    rev = lambda c: (last - c, 0)
    kern = functools.partial(_ssd_bwd_kernel, steps_per_seq=seq // rows)
    return pl.pallas_call(
        kern,
        grid=(T // rows,),
        in_specs=[pl.BlockSpec((rows, D_INNER), rev), pl.BlockSpec((rows, CONV_DIM - D_INNER), rev),
                  pl.BlockSpec((rows, LANES), lambda c: (last - c, 1)),
                  pl.BlockSpec((1, LANES), lambda c: (0, 1)), pl.BlockSpec((1, LANES), lambda c: (0, 1)),
                  _resident(e.shape),
                  pl.BlockSpec((rows, D_INNER), rev),
                  pl.BlockSpec((rows, D_INNER), rev),
                  pl.BlockSpec((rows, D_MODEL), lambda c: (last - c, 2)),
                  pl.BlockSpec((rows, D_MODEL), rev), pl.BlockSpec((rows, D_MODEL), rev),
                  _resident((1, D_INNER)), _resident((1, D_INNER)), _resident((D_INNER, D_MODEL)),
                  _resident((1, D_MODEL)), _resident((D_MODEL, D_MODEL)), _resident((1, D_MODEL))],
        out_specs=pl.BlockSpec((rows, D_MODEL), rev),
        out_shape=jax.ShapeDtypeStruct((T, D_MODEL), F32),
        scratch_shapes=[pltpu.VMEM((SSD_GROUPS, SSD_STATE, GROUP_WIDTH), F32), pltpu.VMEM((rows, D_INNER), BF16)],
        compiler_params=_params(),
        name="ssd_bwd_mix",
    )(xs, bc, dt, dtb, alog, e, yp, zg, zg, ga, x, dskip, nw, wso, bg, wo, pmn)


def _attn_kernel(q_ref, kp_ref, kc_ref, kn_ref, vp_ref, vc_ref, vn_ref, bias_ref, sink_ref, gate_ref, bg_ref,
                 wao_ref, out_ref, *, nb):
    n = pl.program_id(0)
    k_all = jnp.concatenate([kp_ref[...], kc_ref[...], kn_ref[...]], axis=0)
    v_all = jnp.concatenate([vp_ref[...], vc_ref[...], vn_ref[...]], axis=0)
    low_half = lax.broadcasted_iota(jnp.int32, (ATTN_BLOCK, LANES), 1) < HEAD_DIM
    n_tiles = KV_HEADS // 2
    rep = Q_HEADS // KV_HEADS
    half_rows = rep * ATTN_BLOCK

    def head_group(j, t, e):
        blk = n * ATTN_STEP_BLOCKS + j
        variant = ((blk % nb) == 0).astype(jnp.int32) + 2 * ((blk % nb) == nb - 1).astype(jnp.int32)
        kt = k_all[j * ATTN_BLOCK:(j + 3) * ATTN_BLOCK, t * LANES:(t + 1) * LANES]
        vt = v_all[j * ATTN_BLOCK:(j + 3) * ATTN_BLOCK, t * LANES:(t + 1) * LANES]
        keep = low_half if e == 0 else jnp.logical_not(low_half)
        qs = []
        for r in range(rep):
            qt = q_ref[j * ATTN_BLOCK:(j + 1) * ATTN_BLOCK, (rep * t + r) * LANES:(rep * t + r + 1) * LANES]
            qs.append(jnp.where(keep, qt, jnp.zeros_like(qt)))
        qz = jnp.concatenate(qs, axis=0)
        hrows = slice(e * half_rows, (e + 1) * half_rows)
        s = lax.dot_general(qz, kt, _NT, preferred_element_type=F32) + bias_ref[variant, t, hrows, :]
        sk = sink_ref[t, hrows, :]
        m = jnp.maximum(jnp.max(s, axis=-1, keepdims=True), sk)
        p = jnp.exp(s - m).astype(BF16)
        o = jnp.dot(p, jnp.concatenate([vt, jnp.ones_like(vt)], axis=1), preferred_element_type=F32)
        return o[:, :LANES] * (1.0 / (o[:, LANES:] + jnp.exp(sk - m)))

    block_outs = []
    for j in range(ATTN_STEP_BLOCKS):
        tiles = []
        for t in range(n_tiles):
            lo, hi = head_group(j, t, 0), head_group(j, t, 1)
            for r in range(rep):
                rr = slice(r * ATTN_BLOCK, (r + 1) * ATTN_BLOCK)
                tiles.append(jnp.where(low_half, lo[rr], hi[rr]).astype(BF16))
        block_outs.append(jnp.concatenate(tiles, axis=1))
    o = jnp.concatenate(block_outs, axis=0)
    y = jnp.dot(o, wao_ref[...], preferred_element_type=F32)
    out_ref[...] = _sigmoid(gate_ref[...] + bg_ref[...]) * y


def _attn(qkv, bias, sink, zg, bg, wao, seq):
    T = qkv.shape[0]
    nb = seq // ATTN_BLOCK
    nblk = T // ATTN_BLOCK
    rows = ATTN_STEP_BLOCKS * ATTN_BLOCK
    assert seq % rows == 0
    kcol = ATTN_WIDTH // KV_WIDTH
    vcol = kcol + 1
    prev = lambda n: jnp.maximum(n * ATTN_STEP_BLOCKS - 1, 0)
    nxt = lambda n: jnp.minimum((n + 1) * ATTN_STEP_BLOCKS, nblk - 1)
    kern = functools.partial(_attn_kernel, nb=nb)
    return pl.pallas_call(
        kern,
        grid=(T // rows,),
        in_specs=[pl.BlockSpec((rows, ATTN_WIDTH), lambda n: (n, 0)),
                  pl.BlockSpec((ATTN_BLOCK, KV_WIDTH), lambda n: (prev(n), kcol)),
                  pl.BlockSpec((rows, KV_WIDTH), lambda n: (n, kcol)),
                  pl.BlockSpec((ATTN_BLOCK, KV_WIDTH), lambda n: (nxt(n), kcol)),
                  pl.BlockSpec((ATTN_BLOCK, KV_WIDTH), lambda n: (prev(n), vcol)),
                  pl.BlockSpec((rows, KV_WIDTH), lambda n: (n, vcol)),
                  pl.BlockSpec((ATTN_BLOCK, KV_WIDTH), lambda n: (nxt(n), vcol)),
                  _resident(bias.shape), _resident(sink.shape),
                  pl.BlockSpec((rows, D_MODEL), lambda n: (n, 3)),
                  _resident((1, D_MODEL)), _resident((ATTN_WIDTH, D_MODEL))],
        out_specs=pl.BlockSpec((rows, D_MODEL), lambda n: (n, 0)),
        out_shape=jax.ShapeDtypeStruct((T, D_MODEL), F32),
        compiler_params=_params(),
        name="attn",
    )(qkv, qkv, qkv, qkv, qkv, qkv, qkv, bias, sink, zg, bg, wao)


def _mlp_kernel(x_ref, g1_ref, w1_ref, w2_ref, g2_ref, out_ref):
    x = x_ref[...]
    h = _rms(x, g1_ref[...]).astype(BF16)
    step = 1024
    acc = None
    for j in range(D_FF // step):
        a = jnp.dot(h, w1_ref[:, j * step:(j + 1) * step], preferred_element_type=F32)
        a = jnp.square(jnp.maximum(a, 0.0)).astype(BF16)
        t = jnp.dot(a, w2_ref[j * step:(j + 1) * step, :], preferred_element_type=F32)
        acc = t if acc is None else acc + t
    out_ref[...] = x + _rms(acc, g2_ref[...])


def _mlp(x, g1, w1, w2, g2, tm=512):
    T = x.shape[0]
    row = lambda i: (i, 0)
    return pl.pallas_call(
        _mlp_kernel,
        grid=(T // tm,),
        in_specs=[pl.BlockSpec((tm, D_MODEL), row), _resident((1, D_MODEL)), _resident((D_MODEL, D_FF)),
                  _resident((D_FF, D_MODEL)), _resident((1, D_MODEL))],
        out_specs=pl.BlockSpec((tm, D_MODEL), row),
        out_shape=jax.ShapeDtypeStruct((T, D_MODEL), F32),
        compiler_params=_params(),
        name="mlp",
    )(x, g1, w1, w2, g2)


def _permute_heads(a, axis):
    shape = a.shape
    a = a.reshape(shape[:axis] + (KV_HEADS // 2, 2, 4) + shape[axis + 1:])
    a = jnp.swapaxes(a, axis + 1, axis + 2)
    return a.reshape(shape)


def _dt_layout(fwd, bwd, axis=-1):
    return jnp.concatenate([fwd, fwd, fwd, bwd, bwd, bwd, bwd, jnp.zeros_like(bwd)], axis=axis)


def _transpose_cast_kernel(w_ref, o_ref):
    o_ref[...] = w_ref[...].T.astype(BF16)


def _prep_w_in(w_in):
    depth = w_in.shape[0]
    n_in = w_in.shape[2]
    wt = jnp.swapaxes(w_in, 1, 2)
    q0 = KVG_OFF + 2 * SSD_HEADS
    k0 = q0 + ATTN_WIDTH
    rows = 256
    zx_blocks = KVG_OFF // rows
    n_blocks = zx_blocks + (n_in - k0) // rows
    assert KVG_OFF % rows == 0 and (n_in - k0) % rows == 0 and n_in % HALO_ROWS == 0

    def src(l, b):
        row = l * n_in + b * rows + (b >= zx_blocks).astype(jnp.int32) * (k0 - KVG_OFF)
        return (pl.multiple_of(row, HALO_ROWS), 0)

    wbig = pl.pallas_call(
        _transpose_cast_kernel,
        grid=(depth, n_blocks),
        in_specs=[pl.BlockSpec((pl.Element(rows), pl.Element(D_MODEL)), src)],
        out_specs=pl.BlockSpec((None, D_MODEL, rows), lambda l, b: (l, 0, b)),
        out_shape=jax.ShapeDtypeStruct((depth, D_MODEL, n_blocks * rows), BF16),
        compiler_params=_params(2),
        name="weight_transpose",
    )(wt.reshape(depth * n_in, D_MODEL))
    wq = wt[:, q0:k0].reshape(depth, Q_HEADS, HEAD_DIM, D_MODEL) * (HEAD_DIM ** -0.5)
    wq = jnp.swapaxes(_permute_heads(wq, 1).reshape(depth, ATTN_WIDTH, D_MODEL), 1, 2).astype(BF16)
    wdt = _dt_layout(wt[:, KVG_OFF:KVG_OFF + SSD_HEADS], wt[:, KVG_OFF + SSD_HEADS:q0], axis=1)
    wdt = jnp.swapaxes(wdt, 1, 2).astype(BF16)
    return wbig, wq, wdt


def _expand_matrix():
    e = np.zeros((LANES, D_INNER), np.float32)
    for p in range(3):
        for h in range(SSD_HEADS):
            e[p * SSD_HEADS + h, h * SSD_HEAD_DIM:(h + 1) * SSD_HEAD_DIM] = 1.0
    return jnp.asarray(e, BF16)


def _t5_bucket(rel):
    nb = N_BUCKETS // 2
    max_exact = nb // 2
    ret = jnp.where(rel > 0, nb, 0)
    n = jnp.abs(rel)
    nf = jnp.maximum(n, 1).astype(F32)
    large = max_exact + (jnp.log(nf / max_exact) / math.log(MAX_DISTANCE / max_exact)
                         * (nb - max_exact)).astype(jnp.int32)
    large = jnp.minimum(large, nb - 1)
    return ret + jnp.where(n < max_exact, n, large)


def _window_bias(rel_table):
    period = 4 * ATTN_BLOCK
    d = jnp.arange(period)
    d = jnp.where(d >= 3 * ATTN_BLOCK, d - period, d)
    rel = d - ATTN_BLOCK
    vec = rel_table[_t5_bucket(rel)].astype(F32)
    vec = jnp.where((jnp.abs(rel) <= WINDOW)[:, None], vec, -jnp.inf).T
    skew = jnp.tile(vec, (1, ATTN_BLOCK))[:, :ATTN_BLOCK * (period - 1)]
    bias = skew.reshape(Q_HEADS, ATTN_BLOCK, period - 1)[:, :, :3 * ATTN_BLOCK]
    bias = bias.reshape(KV_HEADS // 2, 8 * ATTN_BLOCK, 3 * ATTN_BLOCK)
    col = jnp.arange(3 * ATTN_BLOCK)
    no_prev = jnp.where(col < ATTN_BLOCK, -jnp.inf, 0.0).astype(F32)
    no_next = jnp.where(col >= 2 * ATTN_BLOCK, -jnp.inf, 0.0).astype(F32)
    return jnp.stack([bias, bias + no_prev, bias + no_next, bias + no_prev + no_next])


def _stack_sink(sink):
    return jnp.repeat(sink.astype(F32), ATTN_BLOCK).reshape(KV_HEADS // 2, 8 * ATTN_BLOCK, 1)


def kernel(x, pre_mix_norm, w_in, b_gate, conv_w, conv_b, dt_bias, a_log, d_skip, ssd_norm, w_ssd_out, attn_sink,
           rel_bias_table, w_attn_out, w_o, post_mix_norm, pre_mlp_norm, w_mlp_in, w_mlp_out, post_mlp_norm):
    bsz, seq, d = x.shape
    assert d == D_MODEL and seq % (SSD_STEP_CHUNKS * CHUNK) == 0 and seq % ATTN_BLOCK == 0
    depth = w_in.shape[0]
    xf = x.reshape(bsz * seq, d)
    e = _expand_matrix()
    bias = _window_bias(rel_bias_table)
    r1 = lambda v: v.reshape(1, -1)
    wbig, wq, wdt = _prep_w_in(w_in)
    for l in range(depth):
        xs, bc, zg, qkv, dtr = _inproj(xf, r1(pre_mix_norm[l]), wbig, l, wq[l], wdt[l],
                                       conv_w[l].reshape(CONV_TAPS, CONV_DIM), r1(conv_b[l]), seq)
        dtb = r1(_dt_layout(dt_bias[l, 0], dt_bias[l, 1]).astype(F32))
        alog = r1(_dt_layout(a_log[l, 0], a_log[l, 1]).astype(F32))
        yp = _ssd_fwd(xs, bc, dtr, dtb, alog, e, seq)
        wao = _permute_heads(w_attn_out[l].reshape(Q_HEADS, HEAD_DIM, D_MODEL), 0).reshape(ATTN_WIDTH, D_MODEL)
        ga = _attn(qkv, bias, _stack_sink(attn_sink[l]), zg, r1(b_gate[l, D_MODEL:]), wao.astype(BF16), seq)
        dskip = r1(jnp.repeat(d_skip[l], SSD_HEAD_DIM))
        xf = _ssd_bwd(xs, bc, dtr, dtb, alog, e, yp, zg, ga, xf, dskip, r1(ssd_norm[l]),
                      w_ssd_out[l].astype(BF16), r1(b_gate[l, :D_MODEL]), w_o[l].astype(BF16),
                      r1(post_mix_norm[l]), seq)
        xf = _mlp(xf, r1(pre_mlp_norm[l]), w_mlp_in[l].astype(BF16), w_mlp_out[l].astype(BF16),
                  r1(post_mlp_norm[l]))
    return xf.reshape(bsz, seq, d)
```

```python
import functools
import math

import numpy as np
import jax
import jax.numpy as jnp
from jax import lax
from jax.experimental import pallas as pl
from jax.experimental.pallas import tpu as pltpu

F32 = jnp.float32
BF16 = jnp.bfloat16
EPS = 1e-6

D_MODEL = 1024
D_INNER = 2048
SSD_HEADS = 32
SSD_HEAD_DIM = 64
SSD_GROUPS = 8
HEADS_PER_GROUP = SSD_HEADS // SSD_GROUPS
GROUP_WIDTH = D_INNER // SSD_GROUPS
SSD_STATE = 128
CHUNK = 128
CONV_TAPS = 5
CONV_DIM = D_INNER + 2 * SSD_GROUPS * SSD_STATE
Q_HEADS = 16
KV_HEADS = 4
HEAD_DIM = 64
ATTN_WIDTH = Q_HEADS * HEAD_DIM
KV_WIDTH = KV_HEADS * HEAD_DIM
WINDOW = 128
ATTN_BLOCK = 128
N_BUCKETS = 32
MAX_DISTANCE = 128
D_FF = 4 * D_MODEL
LANES = 128
HALO_ROWS = 8
DT_WIDTH = 2 * LANES
BWD_LANE = 3 * SSD_HEADS

ZG_WIDTH = D_INNER + 2 * D_MODEL
WT_BLOCK = 512
Q_OFF = D_INNER + CONV_DIM + WT_BLOCK
KVG_OFF = Q_OFF + ATTN_WIDTH
QKV_WIDTH = ATTN_WIDTH + 2 * KV_WIDTH
SSD_STEP_CHUNKS = 2
ATTN_STEP_BLOCKS = 2

VMEM_LIMIT = 56 * 1024 * 1024


def _params(n_axes=1, flags=None):
    return pltpu.CompilerParams(dimension_semantics=("arbitrary",) * n_axes,
                                vmem_limit_bytes=VMEM_LIMIT, flags=flags)


def _resident(shape):
    nd = len(shape)
    return pl.BlockSpec(shape, lambda *_: (0,) * nd, pipeline_mode=pl.Buffered(1))


_NT = (((1,), (1,)), ((), ()))
_TN = (((0,), (0,)), ((), ()))


def _rms(x, g):
    ms = jnp.mean(x * x, axis=-1, keepdims=True)
    return x * lax.rsqrt(ms + EPS) * g


def _sigmoid(x):
    return 1.0 / (1.0 + jnp.exp(-x))


def _softplus(x):
    e = jnp.exp(-jnp.abs(x))
    u = 1.0 + e
    log1p_e = jnp.where(u == 1.0, e, jnp.log(u) * (e / (u - 1.0)))
    return jnp.maximum(x, 0.0) + log1p_e


def _inproj_kernel(xp_ref, x_ref, xn_ref, g_ref, wbig_ref, wq_ref, wdt_ref, cw_ref, cb_ref,
                   xs_ref, bc_ref, zg_ref, qkv_ref, dt_ref, ext_ref, *, steps_per_seq, tm, cw):
    i = pl.program_id(0)
    first = (i % steps_per_seq) == 0
    last = (i % steps_per_seq) == steps_per_seq - 1
    g = g_ref[...]
    x = x_ref[...]
    h = _rms(x, g).astype(BF16)
    h_ext = _rms(jnp.concatenate([xp_ref[...], x, xn_ref[...]], axis=0), g).astype(BF16)

    def mm(w_ref, lo, hi):
        return jnp.dot(h, w_ref[:, lo:hi], preferred_element_type=F32)

    sub = lax.broadcasted_iota(jnp.int32, (tm, cw), 0) % HALO_ROWS

    def shifted(slot, d):
        cur = ext_ref[slot, HALO_ROWS:HALO_ROWS + tm, :]
        if d == 0:
            return cur
        if d < 0:
            blend = jnp.where(sub < HALO_ROWS + d, cur, ext_ref[slot, 0:tm, :])
            rot = -d
        else:
            blend = jnp.where(sub >= d, cur, ext_ref[slot, 2 * HALO_ROWS:2 * HALO_ROWS + tm, :])
            rot = HALO_ROWS - d
        blend = blend.reshape(tm // HALO_ROWS, HALO_ROWS, cw)
        return pltpu.roll(blend, rot, axis=1).reshape(tm, cw)

    def project_xbc(cc):
        slot = (cc + i) % 2
        ext = jnp.dot(h_ext, wbig_ref[:, D_INNER + cc * cw:D_INNER + (cc + 1) * cw], preferred_element_type=F32)
        ext_ref[slot, 0:HALO_ROWS, :] = jnp.where(first, 0.0, ext[:HALO_ROWS])
        ext_ref[slot, HALO_ROWS:HALO_ROWS + tm, :] = ext[HALO_ROWS:HALO_ROWS + tm]
        ext_ref[slot, HALO_ROWS + tm:, :] = jnp.where(last, 0.0, ext[HALO_ROWS + tm:])

    def conv_silu(cc):
        sl = slice(cc * cw, (cc + 1) * cw)
        acc = None
        for k in range(CONV_TAPS):
            term = shifted((cc + i) % 2, k - CONV_TAPS // 2) * cw_ref[k:k + 1, sl]
            acc = term if acc is None else acc + term
        acc = acc + cb_ref[:, sl]
        y = acc * _sigmoid(acc)
        if cc * cw < D_INNER:
            xs_ref[:, sl] = y
        else:
            bc_ref[:, cc * cw - D_INNER:(cc + 1) * cw - D_INNER] = y.astype(BF16)

    step = 256

    def z_part(j):
        zg_ref[:, j * step:(j + 1) * step] = mm(wbig_ref, j * step, (j + 1) * step)

    def gate_part(j):
        zg_ref[:, D_INNER + j * step:D_INNER + (j + 1) * step] = mm(
            wbig_ref, KVG_OFF + 2 * KV_WIDTH + j * step, KVG_OFF + 2 * KV_WIDTH + (j + 1) * step)

    def q_part(j):
        qkv_ref[:, j * step:(j + 1) * step] = mm(wq_ref, j * step, (j + 1) * step).astype(BF16)

    def kv_part(j):
        qkv_ref[:, ATTN_WIDTH + j * step:ATTN_WIDTH + (j + 1) * step] = mm(
            wbig_ref, KVG_OFF + j * step, KVG_OFF + (j + 1) * step).astype(BF16)

    def dt_part():
        dt_ref[...] = mm(wdt_ref, 0, DT_WIDTH)

    plain = ([functools.partial(z_part, j) for j in range(D_INNER // step)]
             + [functools.partial(gate_part, j) for j in range(2 * D_MODEL // step)]
             + [functools.partial(q_part, j) for j in range(ATTN_WIDTH // step)]
             + [functools.partial(kv_part, j) for j in range(2 * KV_WIDTH // step)] + [dt_part])
    n_chunks = CONV_DIM // cw
    project_xbc(0)
    for cc in range(n_chunks):
        conv_silu(cc)
        for job in plain[cc * len(plain) // n_chunks:(cc + 1) * len(plain) // n_chunks]:
            job()
        if cc + 1 < n_chunks:
            project_xbc(cc + 1)


def _inproj(x, g, wbig, layer, wq, wdt, conv_w, conv_b, seq, tm=256):
    T = x.shape[0]
    assert seq % tm == 0
    hb = tm // HALO_ROWS
    last_halo = T // HALO_ROWS - 1
    row = lambda i: (i, 0)
    cw = 256
    kern = functools.partial(_inproj_kernel, steps_per_seq=seq // tm, tm=tm, cw=cw)
    return pl.pallas_call(
        kern,
        grid=(T // tm,),
        in_specs=[pl.BlockSpec((HALO_ROWS, D_MODEL), lambda i: (jnp.maximum(i * hb - 1, 0), 0)),
                  pl.BlockSpec((tm, D_MODEL), row),
                  pl.BlockSpec((HALO_ROWS, D_MODEL), lambda i: (jnp.minimum((i + 1) * hb, last_halo), 0)),
                  _resident((1, D_MODEL)),
                  pl.BlockSpec((None,) + wbig.shape[1:], lambda i: (layer, 0, 0), pipeline_mode=pl.Buffered(1)),
                  _resident(wq.shape), _resident(wdt.shape), _resident((CONV_TAPS, CONV_DIM)),
                  _resident((1, CONV_DIM))],
        out_specs=[pl.BlockSpec((tm, D_INNER), row), pl.BlockSpec((tm, CONV_DIM - D_INNER), row),
                   pl.BlockSpec((tm, ZG_WIDTH), row), pl.BlockSpec((tm, QKV_WIDTH), row),
                   pl.BlockSpec((tm, DT_WIDTH), row)],
        out_shape=[jax.ShapeDtypeStruct((T, D_INNER), F32), jax.ShapeDtypeStruct((T, CONV_DIM - D_INNER), BF16),
                   jax.ShapeDtypeStruct((T, ZG_WIDTH), F32), jax.ShapeDtypeStruct((T, QKV_WIDTH), BF16),
                   jax.ShapeDtypeStruct((T, DT_WIDTH), F32)],
        scratch_shapes=[pltpu.VMEM((2, tm + 2 * HALO_ROWS, cw), F32)],
        compiler_params=_params(),
        name="inproj_conv",
    )(x, x, x, g, wbig, wq, wdt, conv_w, conv_b)


def _split3(v):
    hi = v.astype(BF16)
    r1 = v - hi.astype(F32)
    mid = r1.astype(BF16)
    lo = (r1 - mid.astype(F32)).astype(BF16)
    return jnp.concatenate([hi, mid, lo], axis=1)


def _tri_cumsum(adt, upper):
    row = lax.broadcasted_iota(jnp.int32, (CHUNK, CHUNK), 0)
    col = lax.broadcasted_iota(jnp.int32, (CHUNK, CHUNK), 1)
    tri = jnp.where((row <= col) if upper else (row >= col), 1.0, 0.0).astype(BF16)
    p = jnp.dot(tri, _split3(adt), preferred_element_type=F32)
    return p[:, :LANES] + p[:, LANES:2 * LANES] + p[:, 2 * LANES:]


def _pack3(v, lane):
    hi = v.astype(BF16).astype(F32)
    r1 = v - hi
    mid = r1.astype(BF16).astype(F32)
    lo = r1 - mid
    packed = jnp.where(lane < SSD_HEADS, hi,
                       jnp.where(lane < 2 * SSD_HEADS, mid, jnp.where(lane < BWD_LANE, lo, 0.0)))
    return packed.astype(BF16)


def _expand_heads(decay, weight, lane, e_ref):
    packed = jnp.concatenate([_pack3(decay, lane), _pack3(weight, lane)], axis=0)
    ex = jnp.dot(packed, e_ref[...], preferred_element_type=F32)
    return ex[:CHUNK], ex[CHUNK:]


def _ssd_fwd_kernel(xs_ref, bc_ref, dt_ref, dtb_ref, alog_ref, e_ref, y_ref, s_ref, *, steps_per_seq):
    c = pl.program_id(0)

    @pl.when(c % steps_per_seq == 0)
    def _():
        s_ref[...] = jnp.zeros_like(s_ref)

    row = lax.broadcasted_iota(jnp.int32, (CHUNK, CHUNK), 0)
    col = lax.broadcasted_iota(jnp.int32, (CHUNK, CHUNK), 1)
    strict_lower = row > col
    diag = row == col
    lane = lax.broadcasted_iota(jnp.int32, (CHUNK, LANES), 1)
    head_of_lane = lax.broadcasted_iota(jnp.int32, (CHUNK, GROUP_WIDTH), 1) // SSD_HEAD_DIM
    a = -jnp.exp(alog_ref[...])
    nb = SSD_GROUPS * SSD_STATE
    for ci in range(SSD_STEP_CHUNKS):
        rows = slice(ci * CHUNK, (ci + 1) * CHUNK)
        dt = _softplus(dt_ref[rows, :] + dtb_ref[...])
        adt = dt * a
        acs = jnp.where(lane < BWD_LANE, _tri_cumsum(adt, False), _tri_cumsum(adt, True))
        q_t = (acs - jnp.log(dt)).T
        dt_t = dt.T
        log_dt_sum_t = jnp.log(dt_t[0:SSD_HEADS] + dt_t[BWD_LANE:BWD_LANE + SSD_HEADS])
        decay_in, w_state = _expand_heads(jnp.exp(acs), jnp.exp(acs[CHUNK - 1:CHUNK, :] - acs) * dt, lane, e_ref)
        xs = xs_ref[rows, :]
        xw = (xs * w_state).astype(BF16)
        xb = xs.astype(BF16)
        for g in range(SSD_GROUPS):
            gs = slice(g * GROUP_WIDTH, (g + 1) * GROUP_WIDTH)
            b_g = bc_ref[rows, g * SSD_STATE:(g + 1) * SSD_STATE]
            c_g = bc_ref[rows, nb + g * SSD_STATE:nb + (g + 1) * SSD_STATE]
            cb = lax.dot_general(c_g, b_g, _NT, preferred_element_type=F32)
            x_g = xb[:, gs]
            ms, xm = [], []
            for r in range(HEADS_PER_GROUP):
                hf = g * HEADS_PER_GROUP + r
                hb = BWD_LANE + hf
                arg = jnp.where(strict_lower, acs[:, hf:hf + 1] - q_t[hf:hf + 1, :],
                                jnp.where(diag, log_dt_sum_t[hf:hf + 1, :], acs[:, hb:hb + 1] - q_t[hb:hb + 1, :]))
                ms.append((cb * jnp.exp(arg)).astype(BF16))
                xm.append(jnp.where(head_of_lane == r, x_g, jnp.zeros_like(x_g)))
            y_diag = (jnp.dot(jnp.concatenate(ms[0:2], axis=1), jnp.concatenate(xm[0:2], axis=0),
                              preferred_element_type=F32)
                      + jnp.dot(jnp.concatenate(ms[2:4], axis=1), jnp.concatenate(xm[2:4], axis=0),
                                preferred_element_type=F32))
            s_g = s_ref[g]
            y_off = jnp.dot(c_g, s_g.astype(BF16), preferred_element_type=F32) * decay_in[:, gs]
            y_ref[rows, gs] = y_diag + y_off
            upd = lax.dot_general(b_g, xw[:, gs], _TN, preferred_element_type=F32)
            s_ref[g] = s_g * decay_in[CHUNK - 1:CHUNK, gs] + upd


def _ssd_fwd(xs, bc, dt, dtb, alog, e, seq):
    T = xs.shape[0]
    rows = SSD_STEP_CHUNKS * CHUNK
    assert seq % rows == 0
    row = lambda c: (c, 0)
    kern = functools.partial(_ssd_fwd_kernel, steps_per_seq=seq // rows)
    return pl.pallas_call(
        kern,
        grid=(T // rows,),
        in_specs=[pl.BlockSpec((rows, D_INNER), row), pl.BlockSpec((rows, CONV_DIM - D_INNER), row),
                  pl.BlockSpec((rows, LANES), row), pl.BlockSpec((1, LANES), lambda c: (0, 0)),
                  pl.BlockSpec((1, LANES), lambda c: (0, 0)), _resident(e.shape)],
        out_specs=pl.BlockSpec((rows, D_INNER), row),
        out_shape=jax.ShapeDtypeStruct((T, D_INNER), F32),
        scratch_shapes=[pltpu.VMEM((SSD_GROUPS, SSD_STATE, GROUP_WIDTH), F32)],
        compiler_params=_params(),
        name="ssd_fwd",
    )(xs, bc, dt, dtb, alog, e)


def _ssd_bwd_kernel(xs_ref, bc_ref, dt_ref, dtb_ref, alog_ref, e_ref, yp_ref, z_ref, gs_ref, ga_ref, x_ref,
                    dskip_ref, nw_ref, wso_ref, bg_ref, wo_ref, pmn_ref, out_ref, s_ref, yn_ref, *,
                    steps_per_seq):
    c = pl.program_id(0)

    @pl.when(c % steps_per_seq == 0)
    def _():
        s_ref[...] = jnp.zeros_like(s_ref)

    lane = lax.broadcasted_iota(jnp.int32, (CHUNK, LANES), 1)
    a = -jnp.exp(alog_ref[...])
    nb = SSD_GROUPS * SSD_STATE
    for ci in reversed(range(SSD_STEP_CHUNKS)):
        rows = slice(ci * CHUNK, (ci + 1) * CHUNK)
        dt = _softplus(dt_ref[rows, :] + dtb_ref[...])
        acs = _tri_cumsum(dt * a, True)
        decay_in, w_state = _expand_heads(jnp.exp(acs), jnp.exp(acs[0:1, :] - acs) * dt, lane, e_ref)
        xs = xs_ref[rows, :]
        xw = (xs * w_state).astype(BF16)
        for g in range(SSD_GROUPS):
            gs = slice(g * GROUP_WIDTH, (g + 1) * GROUP_WIDTH)
            b_g = bc_ref[rows, g * SSD_STATE:(g + 1) * SSD_STATE]
            c_g = bc_ref[rows, nb + g * SSD_STATE:nb + (g + 1) * SSD_STATE]
            s_g = s_ref[g]
            y_off = jnp.dot(c_g, s_g.astype(BF16), preferred_element_type=F32) * decay_in[:, gs]
            y = yp_ref[rows, gs] + y_off + xs[:, gs] * dskip_ref[:, gs]
            z = z_ref[rows, gs]
            u = y * (z * _sigmoid(z))
            yn_ref[rows, gs] = _rms(u, nw_ref[:, gs]).astype(BF16)
            upd = lax.dot_general(b_g, xw[:, gs], _TN, preferred_element_type=F32)
            s_ref[g] = s_g * decay_in[0:1, gs] + upd
    y_ssd = jnp.dot(yn_ref[...], wso_ref[...], preferred_element_type=F32)
    gate = _sigmoid(gs_ref[...] + bg_ref[...])
    mix = (gate * y_ssd + ga_ref[...]).astype(BF16)
    mixed = jnp.dot(mix, wo_ref[...], preferred_element_type=F32)
    out_ref[...] = x_ref[...] + _rms(mixed, pmn_ref[...])


def _ssd_bwd(xs, bc, dt, dtb, alog, e, yp, zg, ga, x, dskip, nw, wso, bg, wo, pmn, seq):
    T = xs.shape[0]
    rows = SSD_STEP_CHUNKS * CHUNK
    last = T // rows - 1
    rev = lambda c: (last - c, 0)
    kern = functools.partial(_ssd_bwd_kernel, steps_per_seq=seq // rows)
    return pl.pallas_call(
        kern,
        grid=(T // rows,),
        in_specs=[pl.BlockSpec((rows, D_INNER), rev), pl.BlockSpec((rows, CONV_DIM - D_INNER), rev),
                  pl.BlockSpec((rows, LANES), lambda c: (last - c, 1)),
                  pl.BlockSpec((1, LANES), lambda c: (0, 1)), pl.BlockSpec((1, LANES), lambda c: (0, 1)),
                  _resident(e.shape),
                  pl.BlockSpec((rows, D_INNER), rev),
                  pl.BlockSpec((rows, D_INNER), rev),
                  pl.BlockSpec((rows, D_MODEL), lambda c: (last - c, 2)),
                  pl.BlockSpec((rows, D_MODEL), rev), pl.BlockSpec((rows, D_MODEL), rev),
                  _resident((1, D_INNER)), _resident((1, D_INNER)), _resident((D_INNER, D_MODEL)),
                  _resident((1, D_MODEL)), _resident((D_MODEL, D_MODEL)), _resident((1, D_MODEL))],
        out_specs=pl.BlockSpec((rows, D_MODEL), rev),
        out_shape=jax.ShapeDtypeStruct((T, D_MODEL), F32),
        scratch_shapes=[pltpu.VMEM((SSD_GROUPS, SSD_STATE, GROUP_WIDTH), F32), pltpu.VMEM((rows, D_INNER), BF16)],
        compiler_params=_params(),
        name="ssd_bwd_mix",
    )(xs, bc, dt, dtb, alog, e, yp, zg, zg, ga, x, dskip, nw, wso, bg, wo, pmn)


def _attn_kernel(q_ref, kp_ref, kc_ref, kn_ref, vp_ref, vc_ref, vn_ref, bias_ref, sink_ref, gate_ref, bg_ref,
                 wao_ref, out_ref, *, nb):
    n = pl.program_id(0)
    k_all = jnp.concatenate([kp_ref[...], kc_ref[...], kn_ref[...]], axis=0)
    v_all = jnp.concatenate([vp_ref[...], vc_ref[...], vn_ref[...]], axis=0)
    low_half = lax.broadcasted_iota(jnp.int32, (ATTN_BLOCK, LANES), 1) < HEAD_DIM
    n_tiles = KV_HEADS // 2
    rep = Q_HEADS // KV_HEADS
    half_rows = rep * ATTN_BLOCK

    def head_group(j, t, e):
        blk = n * ATTN_STEP_BLOCKS + j
        variant = ((blk % nb) == 0).astype(jnp.int32) + 2 * ((blk % nb) == nb - 1).astype(jnp.int32)
        kt = k_all[j * ATTN_BLOCK:(j + 3) * ATTN_BLOCK, t * LANES:(t + 1) * LANES]
        vt = v_all[j * ATTN_BLOCK:(j + 3) * ATTN_BLOCK, t * LANES:(t + 1) * LANES]
        keep = low_half if e == 0 else jnp.logical_not(low_half)
        qs = []
        for r in range(rep):
            qt = q_ref[j * ATTN_BLOCK:(j + 1) * ATTN_BLOCK, (rep * t + r) * LANES:(rep * t + r + 1) * LANES]
            qs.append(jnp.where(keep, qt, jnp.zeros_like(qt)))
        qz = jnp.concatenate(qs, axis=0)
        hrows = slice(e * half_rows, (e + 1) * half_rows)
        s = lax.dot_general(qz, kt, _NT, preferred_element_type=F32) + bias_ref[variant, t, hrows, :]
        sk = sink_ref[t, hrows, :]
        m = jnp.maximum(jnp.max(s, axis=-1, keepdims=True), sk)
        p = jnp.exp(s - m).astype(BF16)
        o = jnp.dot(p, jnp.concatenate([vt, jnp.ones_like(vt)], axis=1), preferred_element_type=F32)
        return o[:, :LANES] * (1.0 / (o[:, LANES:] + jnp.exp(sk - m)))

    block_outs = []
    for j in range(ATTN_STEP_BLOCKS):
        tiles = []
        for t in range(n_tiles):
            lo, hi = head_group(j, t, 0), head_group(j, t, 1)
            for r in range(rep):
                rr = slice(r * ATTN_BLOCK, (r + 1) * ATTN_BLOCK)
                tiles.append(jnp.where(low_half, lo[rr], hi[rr]).astype(BF16))
        block_outs.append(jnp.concatenate(tiles, axis=1))
    o = jnp.concatenate(block_outs, axis=0)
    y = jnp.dot(o, wao_ref[...], preferred_element_type=F32)
    out_ref[...] = _sigmoid(gate_ref[...] + bg_ref[...]) * y


def _attn(qkv, bias, sink, zg, bg, wao, seq):
    T = qkv.shape[0]
    nb = seq // ATTN_BLOCK
    nblk = T // ATTN_BLOCK
    rows = ATTN_STEP_BLOCKS * ATTN_BLOCK
    assert seq % rows == 0
    kcol = ATTN_WIDTH // KV_WIDTH
    vcol = kcol + 1
    prev = lambda n: jnp.maximum(n * ATTN_STEP_BLOCKS - 1, 0)
    nxt = lambda n: jnp.minimum((n + 1) * ATTN_STEP_BLOCKS, nblk - 1)
    kern = functools.partial(_attn_kernel, nb=nb)
    return pl.pallas_call(
        kern,
        grid=(T // rows,),
        in_specs=[pl.BlockSpec((rows, ATTN_WIDTH), lambda n: (n, 0)),
                  pl.BlockSpec((ATTN_BLOCK, KV_WIDTH), lambda n: (prev(n), kcol)),
                  pl.BlockSpec((rows, KV_WIDTH), lambda n: (n, kcol)),
                  pl.BlockSpec((ATTN_BLOCK, KV_WIDTH), lambda n: (nxt(n), kcol)),
                  pl.BlockSpec((ATTN_BLOCK, KV_WIDTH), lambda n: (prev(n), vcol)),
                  pl.BlockSpec((rows, KV_WIDTH), lambda n: (n, vcol)),
                  pl.BlockSpec((ATTN_BLOCK, KV_WIDTH), lambda n: (nxt(n), vcol)),
                  _resident(bias.shape), _resident(sink.shape),
                  pl.BlockSpec((rows, D_MODEL), lambda n: (n, 3)),
                  _resident((1, D_MODEL)), _resident((ATTN_WIDTH, D_MODEL))],
        out_specs=pl.BlockSpec((rows, D_MODEL), lambda n: (n, 0)),
        out_shape=jax.ShapeDtypeStruct((T, D_MODEL), F32),
        compiler_params=_params(),
        name="attn",
    )(qkv, qkv, qkv, qkv, qkv, qkv, qkv, bias, sink, zg, bg, wao)


def _mlp_kernel(x_ref, g1_ref, w1_ref, w2_ref, g2_ref, out_ref):
    x = x_ref[...]
    h = _rms(x, g1_ref[...]).astype(BF16)
    step = 1024
    acc = None
    for j in range(D_FF // step):
        a = jnp.dot(h, w1_ref[:, j * step:(j + 1) * step], preferred_element_type=F32)
        a = jnp.square(jnp.maximum(a, 0.0)).astype(BF16)
        t = jnp.dot(a, w2_ref[j * step:(j + 1) * step, :], preferred_element_type=F32)
        acc = t if acc is None else acc + t
    out_ref[...] = x + _rms(acc, g2_ref[...])


def _mlp(x, g1, w1, w2, g2, tm=512):
    T = x.shape[0]
    row = lambda i: (i, 0)
    return pl.pallas_call(
        _mlp_kernel,
        grid=(T // tm,),
        in_specs=[pl.BlockSpec((tm, D_MODEL), row), _resident((1, D_MODEL)), _resident((D_MODEL, D_FF)),
                  _resident((D_FF, D_MODEL)), _resident((1, D_MODEL))],
        out_specs=pl.BlockSpec((tm, D_MODEL), row),
        out_shape=jax.ShapeDtypeStruct((T, D_MODEL), F32),
        compiler_params=_params(),
        name="mlp",
    )(x, g1, w1, w2, g2)


def _permute_heads(a, axis):
    shape = a.shape
    a = a.reshape(shape[:axis] + (KV_HEADS // 2, 2, 4) + shape[axis + 1:])
    a = jnp.swapaxes(a, axis + 1, axis + 2)
    return a.reshape(shape)


def _dt_layout(fwd, bwd, axis=-1):
    return jnp.concatenate([fwd, fwd, fwd, bwd, bwd, bwd, bwd, jnp.zeros_like(bwd)], axis=axis)


def _transpose_cast_kernel(w_ref, o_ref):
    o_ref[...] = w_ref[...].T.astype(BF16)


def _prep_w_in(w_in):
    depth = w_in.shape[0]
    n_in = w_in.shape[2]
    wt = jnp.swapaxes(w_in, 1, 2)
    zx = D_INNER + CONV_DIM
    q0 = zx + 2 * SSD_HEADS
    k0 = q0 + ATTN_WIDTH
    rows = WT_BLOCK
    n_blocks = (zx + rows + ATTN_WIDTH + n_in - k0) // rows
    assert zx % rows == 0 and ATTN_WIDTH % rows == 0 and (n_in - k0) % rows == 0 and n_in % HALO_ROWS == 0

    def src(l, b):
        row = l * n_in + b * rows - (b > zx // rows).astype(jnp.int32) * (zx + rows - q0)
        return (pl.multiple_of(row, HALO_ROWS), 0)

    wbig = pl.pallas_call(
        _transpose_cast_kernel,
        grid=(depth, n_blocks),
        in_specs=[pl.BlockSpec((pl.Element(rows), pl.Element(D_MODEL)), src)],
        out_specs=pl.BlockSpec((None, D_MODEL, rows), lambda l, b: (l, 0, b)),
        out_shape=jax.ShapeDtypeStruct((depth, D_MODEL, n_blocks * rows), BF16),
        compiler_params=_params(2),
        name="weight_transpose",
    )(wt.reshape(depth * n_in, D_MODEL))
    wq = wbig[:, :, Q_OFF:KVG_OFF].reshape(depth, D_MODEL, Q_HEADS, HEAD_DIM) * (HEAD_DIM ** -0.5)
    wq = _permute_heads(wq, 2).reshape(depth, D_MODEL, ATTN_WIDTH).astype(BF16)
    wdt = _dt_layout(wbig[:, :, zx:zx + SSD_HEADS], wbig[:, :, zx + SSD_HEADS:q0])
    return wbig, wq, wdt


def _expand_matrix():
    e = np.zeros((LANES, D_INNER), np.float32)
    for p in range(3):
        for h in range(SSD_HEADS):
            e[p * SSD_HEADS + h, h * SSD_HEAD_DIM:(h + 1) * SSD_HEAD_DIM] = 1.0
    return jnp.asarray(e, BF16)


def _t5_bucket(rel):
    nb = N_BUCKETS // 2
    max_exact = nb // 2
    ret = jnp.where(rel > 0, nb, 0)
    n = jnp.abs(rel)
    nf = jnp.maximum(n, 1).astype(F32)
    large = max_exact + (jnp.log(nf / max_exact) / math.log(MAX_DISTANCE / max_exact)
                         * (nb - max_exact)).astype(jnp.int32)
    large = jnp.minimum(large, nb - 1)
    return ret + jnp.where(n < max_exact, n, large)


def _window_bias(rel_table):
    period = 4 * ATTN_BLOCK
    d = jnp.arange(period)
    d = jnp.where(d >= 3 * ATTN_BLOCK, d - period, d)
    rel = d - ATTN_BLOCK
    vec = rel_table[_t5_bucket(rel)].astype(F32)
    vec = jnp.where((jnp.abs(rel) <= WINDOW)[:, None], vec, -jnp.inf).T
    skew = jnp.tile(vec, (1, ATTN_BLOCK))[:, :ATTN_BLOCK * (period - 1)]
    bias = skew.reshape(Q_HEADS, ATTN_BLOCK, period - 1)[:, :, :3 * ATTN_BLOCK]
    bias = bias.reshape(KV_HEADS // 2, 8 * ATTN_BLOCK, 3 * ATTN_BLOCK)
    col = jnp.arange(3 * ATTN_BLOCK)
    no_prev = jnp.where(col < ATTN_BLOCK, -jnp.inf, 0.0).astype(F32)
    no_next = jnp.where(col >= 2 * ATTN_BLOCK, -jnp.inf, 0.0).astype(F32)
    return jnp.stack([bias, bias + no_prev, bias + no_next, bias + no_prev + no_next])


def _stack_sink(sink):
    return jnp.repeat(sink.astype(F32), ATTN_BLOCK).reshape(KV_HEADS // 2, 8 * ATTN_BLOCK, 1)


def kernel(x, pre_mix_norm, w_in, b_gate, conv_w, conv_b, dt_bias, a_log, d_skip, ssd_norm, w_ssd_out, attn_sink,
           rel_bias_table, w_attn_out, w_o, post_mix_norm, pre_mlp_norm, w_mlp_in, w_mlp_out, post_mlp_norm):
    bsz, seq, d = x.shape
    assert d == D_MODEL and seq % (SSD_STEP_CHUNKS * CHUNK) == 0 and seq % ATTN_BLOCK == 0
    depth = w_in.shape[0]
    xf = x.reshape(bsz * seq, d)
    e = _expand_matrix()
    bias = _window_bias(rel_bias_table)
    r1 = lambda v: v.reshape(1, -1)
    wbig, wq, wdt = _prep_w_in(w_in)
    for l in range(depth):
        xs, bc, zg, qkv, dtr = _inproj(xf, r1(pre_mix_norm[l]), wbig, l, wq[l], wdt[l],
                                       conv_w[l].reshape(CONV_TAPS, CONV_DIM), r1(conv_b[l]), seq)
        dtb = r1(_dt_layout(dt_bias[l, 0], dt_bias[l, 1]).astype(F32))
        alog = r1(_dt_layout(a_log[l, 0], a_log[l, 1]).astype(F32))
        yp = _ssd_fwd(xs, bc, dtr, dtb, alog, e, seq)
        wao = _permute_heads(w_attn_out[l].reshape(Q_HEADS, HEAD_DIM, D_MODEL), 0).reshape(ATTN_WIDTH, D_MODEL)
        ga = _attn(qkv, bias, _stack_sink(attn_sink[l]), zg, r1(b_gate[l, D_MODEL:]), wao.astype(BF16), seq)
        dskip = r1(jnp.repeat(d_skip[l], SSD_HEAD_DIM))
        xf = _ssd_bwd(xs, bc, dtr, dtb, alog, e, yp, zg, ga, xf, dskip, r1(ssd_norm[l]),
                      w_ssd_out[l].astype(BF16), r1(b_gate[l, :D_MODEL]), w_o[l].astype(BF16),
                      r1(post_mix_norm[l]), seq)
        xf = _mlp(xf, r1(pre_mlp_norm[l]), w_mlp_in[l].astype(BF16), w_mlp_out[l].astype(BF16),
                  r1(post_mlp_norm[l]))
    return xf.reshape(bsz, seq, d)
```

```python
import functools
import math

import numpy as np
import jax
import jax.numpy as jnp
from jax import lax
from jax.experimental import pallas as pl
from jax.experimental.pallas import tpu as pltpu

F32 = jnp.float32
BF16 = jnp.bfloat16
EPS = 1e-6

D_MODEL = 1024
D_INNER = 2048
SSD_HEADS = 32
SSD_HEAD_DIM = 64
SSD_GROUPS = 8
HEADS_PER_GROUP = SSD_HEADS // SSD_GROUPS
GROUP_WIDTH = D_INNER // SSD_GROUPS
SSD_STATE = 128
CHUNK = 128
CONV_TAPS = 5
CONV_DIM = D_INNER + 2 * SSD_GROUPS * SSD_STATE
Q_HEADS = 16
KV_HEADS = 4
HEAD_DIM = 64
ATTN_WIDTH = Q_HEADS * HEAD_DIM
KV_WIDTH = KV_HEADS * HEAD_DIM
WINDOW = 128
ATTN_BLOCK = 128
N_BUCKETS = 32
MAX_DISTANCE = 128
D_FF = 4 * D_MODEL
LANES = 128
HALO_ROWS = 8
DT_WIDTH = 2 * LANES
BWD_LANE = 3 * SSD_HEADS

ZG_WIDTH = D_INNER + 2 * D_MODEL
WT_BLOCK = 512
Q_OFF = D_INNER + CONV_DIM + WT_BLOCK
KVG_OFF = Q_OFF + ATTN_WIDTH
QKV_WIDTH = ATTN_WIDTH + 2 * KV_WIDTH
SSD_STEP_CHUNKS = 2
ATTN_STEP_BLOCKS = 2

VMEM_LIMIT = 56 * 1024 * 1024


def _params(n_axes=1, flags=None):
    return pltpu.CompilerParams(dimension_semantics=("arbitrary",) * n_axes,
                                vmem_limit_bytes=VMEM_LIMIT, flags=flags)


def _resident(shape):
    nd = len(shape)
    return pl.BlockSpec(shape, lambda *_: (0,) * nd, pipeline_mode=pl.Buffered(1))


def _of_layer(a, layer, block=None, col=0):
    tail = tuple(a.shape[1:]) if block is None else tuple(block)
    return pl.BlockSpec((None,) + tail, lambda *_: (layer,) + (0,) * (len(tail) - 1) + (col,),
                        pipeline_mode=pl.Buffered(1))


_NT = (((1,), (1,)), ((), ()))
_TN = (((0,), (0,)), ((), ()))


def _rms(x, g):
    ms = jnp.mean(x * x, axis=-1, keepdims=True)
    return x * lax.rsqrt(ms + EPS) * g


def _sigmoid(x):
    return 1.0 / (1.0 + jnp.exp(-x))


def _softplus(x):
    e = jnp.exp(-jnp.abs(x))
    u = 1.0 + e
    log1p_e = jnp.where(u == 1.0, e, jnp.log(u) * (e / (u - 1.0)))
    return jnp.maximum(x, 0.0) + log1p_e


def _inproj_kernel(xp_ref, x_ref, xn_ref, g_ref, wbig_ref, wq_ref, wdt_ref, cw_ref, cb_ref,
                   xs_ref, bc_ref, zg_ref, qkv_ref, dt_ref, ext_ref, *, steps_per_seq, tm, cw):
    i = pl.program_id(0)
    first = (i % steps_per_seq) == 0
    last = (i % steps_per_seq) == steps_per_seq - 1
    g = g_ref[...]
    x = x_ref[...]
    h = _rms(x, g).astype(BF16)
    h_ext = _rms(jnp.concatenate([xp_ref[...], x, xn_ref[...]], axis=0), g).astype(BF16)

    def mm(w_ref, lo, hi):
        return jnp.dot(h, w_ref[:, lo:hi], preferred_element_type=F32)

    sub = lax.broadcasted_iota(jnp.int32, (tm, cw), 0) % HALO_ROWS

    def shifted(slot, d):
        cur = ext_ref[slot, HALO_ROWS:HALO_ROWS + tm, :]
        if d == 0:
            return cur
        if d < 0:
            blend = jnp.where(sub < HALO_ROWS + d, cur, ext_ref[slot, 0:tm, :])
            rot = -d
        else:
            blend = jnp.where(sub >= d, cur, ext_ref[slot, 2 * HALO_ROWS:2 * HALO_ROWS + tm, :])
            rot = HALO_ROWS - d
        blend = blend.reshape(tm // HALO_ROWS, HALO_ROWS, cw)
        return pltpu.roll(blend, rot, axis=1).reshape(tm, cw)

    def project_xbc(cc):
        slot = (cc + i) % 2
        ext = jnp.dot(h_ext, wbig_ref[:, D_INNER + cc * cw:D_INNER + (cc + 1) * cw], preferred_element_type=F32)
        ext_ref[slot, 0:HALO_ROWS, :] = jnp.where(first, 0.0, ext[:HALO_ROWS])
        ext_ref[slot, HALO_ROWS:HALO_ROWS + tm, :] = ext[HALO_ROWS:HALO_ROWS + tm]
        ext_ref[slot, HALO_ROWS + tm:, :] = jnp.where(last, 0.0, ext[HALO_ROWS + tm:])

    def conv_silu(cc):
        sl = slice(cc * cw, (cc + 1) * cw)
        acc = None
        for k in range(CONV_TAPS):
            term = shifted((cc + i) % 2, k - CONV_TAPS // 2) * cw_ref[k:k + 1, sl]
            acc = term if acc is None else acc + term
        acc = acc + cb_ref[:, sl]
        y = acc * _sigmoid(acc)
        if cc * cw < D_INNER:
            xs_ref[:, sl] = y
        else:
            bc_ref[:, cc * cw - D_INNER:(cc + 1) * cw - D_INNER] = y.astype(BF16)

    step = 256

    def z_part(j):
        zg_ref[:, j * step:(j + 1) * step] = mm(wbig_ref, j * step, (j + 1) * step)

    def gate_part(j):
        zg_ref[:, D_INNER + j * step:D_INNER + (j + 1) * step] = mm(
            wbig_ref, KVG_OFF + 2 * KV_WIDTH + j * step, KVG_OFF + 2 * KV_WIDTH + (j + 1) * step)

    def q_part(j):
        qkv_ref[:, j * step:(j + 1) * step] = mm(wq_ref, j * step, (j + 1) * step).astype(BF16)

    def kv_part(j):
        qkv_ref[:, ATTN_WIDTH + j * step:ATTN_WIDTH + (j + 1) * step] = mm(
            wbig_ref, KVG_OFF + j * step, KVG_OFF + (j + 1) * step).astype(BF16)

    def dt_part():
        dt_ref[...] = mm(wdt_ref, 0, DT_WIDTH)

    plain = ([functools.partial(z_part, j) for j in range(D_INNER // step)]
             + [functools.partial(gate_part, j) for j in range(2 * D_MODEL // step)]
             + [functools.partial(q_part, j) for j in range(ATTN_WIDTH // step)]
             + [functools.partial(kv_part, j) for j in range(2 * KV_WIDTH // step)] + [dt_part])
    n_chunks = CONV_DIM // cw
    project_xbc(0)
    for cc in range(n_chunks):
        conv_silu(cc)
        for job in plain[cc * len(plain) // n_chunks:(cc + 1) * len(plain) // n_chunks]:
            job()
        if cc + 1 < n_chunks:
            project_xbc(cc + 1)


def _inproj(x, g, wbig, layer, wq, wdt, conv_w, conv_b, seq, tm=256):
    T = x.shape[0]
    assert seq % tm == 0
    hb = tm // HALO_ROWS
    last_halo = T // HALO_ROWS - 1
    row = lambda i: (i, 0)
    cw = 256
    kern = functools.partial(_inproj_kernel, steps_per_seq=seq // tm, tm=tm, cw=cw)
    return pl.pallas_call(
        kern,
        grid=(T // tm,),
        in_specs=[pl.BlockSpec((HALO_ROWS, D_MODEL), lambda i: (jnp.maximum(i * hb - 1, 0), 0)),
                  pl.BlockSpec((tm, D_MODEL), row),
                  pl.BlockSpec((HALO_ROWS, D_MODEL), lambda i: (jnp.minimum((i + 1) * hb, last_halo), 0)),
                  _of_layer(g, layer), _of_layer(wbig, layer), _of_layer(wq, layer), _of_layer(wdt, layer),
                  _of_layer(conv_w, layer), _of_layer(conv_b, layer)],
        out_specs=[pl.BlockSpec((tm, D_INNER), row), pl.BlockSpec((tm, CONV_DIM - D_INNER), row),
                   pl.BlockSpec((tm, ZG_WIDTH), row), pl.BlockSpec((tm, QKV_WIDTH), row),
                   pl.BlockSpec((tm, DT_WIDTH), row)],
        out_shape=[jax.ShapeDtypeStruct((T, D_INNER), F32), jax.ShapeDtypeStruct((T, CONV_DIM - D_INNER), BF16),
                   jax.ShapeDtypeStruct((T, ZG_WIDTH), F32), jax.ShapeDtypeStruct((T, QKV_WIDTH), BF16),
                   jax.ShapeDtypeStruct((T, DT_WIDTH), F32)],
        scratch_shapes=[pltpu.VMEM((2, tm + 2 * HALO_ROWS, cw), F32)],
        compiler_params=_params(),
        name="inproj_conv",
    )(x, x, x, g, wbig, wq, wdt, conv_w, conv_b)


def _split3(v):
    hi = v.astype(BF16)
    r1 = v - hi.astype(F32)
    mid = r1.astype(BF16)
    lo = (r1 - mid.astype(F32)).astype(BF16)
    return jnp.concatenate([hi, mid, lo], axis=1)


def _tri_cumsum(adt, upper):
    row = lax.broadcasted_iota(jnp.int32, (CHUNK, CHUNK), 0)
    col = lax.broadcasted_iota(jnp.int32, (CHUNK, CHUNK), 1)
    tri = jnp.where((row <= col) if upper else (row >= col), 1.0, 0.0).astype(BF16)
    p = jnp.dot(tri, _split3(adt), preferred_element_type=F32)
    return p[:, :LANES] + p[:, LANES:2 * LANES] + p[:, 2 * LANES:]


def _pack3(v, lane):
    hi = v.astype(BF16).astype(F32)
    r1 = v - hi
    mid = r1.astype(BF16).astype(F32)
    lo = r1 - mid
    packed = jnp.where(lane < SSD_HEADS, hi,
                       jnp.where(lane < 2 * SSD_HEADS, mid, jnp.where(lane < BWD_LANE, lo, 0.0)))
    return packed.astype(BF16)


def _expand_heads(decay, weight, lane, e_ref):
    packed = jnp.concatenate([_pack3(decay, lane), _pack3(weight, lane)], axis=0)
    ex = jnp.dot(packed, e_ref[...], preferred_element_type=F32)
    return ex[:CHUNK], ex[CHUNK:]


def _ssd_fwd_kernel(xs_ref, bc_ref, dt_ref, dtb_ref, alog_ref, e_ref, y_ref, s_ref, *, steps_per_seq):
    c = pl.program_id(0)

    @pl.when(c % steps_per_seq == 0)
    def _():
        s_ref[...] = jnp.zeros_like(s_ref)

    row = lax.broadcasted_iota(jnp.int32, (CHUNK, CHUNK), 0)
    col = lax.broadcasted_iota(jnp.int32, (CHUNK, CHUNK), 1)
    strict_lower = row > col
    diag = row == col
    lane = lax.broadcasted_iota(jnp.int32, (CHUNK, LANES), 1)
    head_of_lane = lax.broadcasted_iota(jnp.int32, (CHUNK, GROUP_WIDTH), 1) // SSD_HEAD_DIM
    a = -jnp.exp(alog_ref[...])
    nb = SSD_GROUPS * SSD_STATE
    for ci in range(SSD_STEP_CHUNKS):
        rows = slice(ci * CHUNK, (ci + 1) * CHUNK)
        dt = _softplus(dt_ref[rows, :] + dtb_ref[...])
        adt = dt * a
        acs = jnp.where(lane < BWD_LANE, _tri_cumsum(adt, False), _tri_cumsum(adt, True))
        q_t = (acs - jnp.log(dt)).T
        dt_t = dt.T
        log_dt_sum_t = jnp.log(dt_t[0:SSD_HEADS] + dt_t[BWD_LANE:BWD_LANE + SSD_HEADS])
        decay_in, w_state = _expand_heads(jnp.exp(acs), jnp.exp(acs[CHUNK - 1:CHUNK, :] - acs) * dt, lane, e_ref)
        xs = xs_ref[rows, :]
        xw = (xs * w_state).astype(BF16)
        xb = xs.astype(BF16)
        for g in range(SSD_GROUPS):
            gs = slice(g * GROUP_WIDTH, (g + 1) * GROUP_WIDTH)
            b_g = bc_ref[rows, g * SSD_STATE:(g + 1) * SSD_STATE]
            c_g = bc_ref[rows, nb + g * SSD_STATE:nb + (g + 1) * SSD_STATE]
            cb = lax.dot_general(c_g, b_g, _NT, preferred_element_type=F32)
            x_g = xb[:, gs]
            ms, xm = [], []
            for r in range(HEADS_PER_GROUP):
                hf = g * HEADS_PER_GROUP + r
                hb = BWD_LANE + hf
                arg = jnp.where(strict_lower, acs[:, hf:hf + 1] - q_t[hf:hf + 1, :],
                                jnp.where(diag, log_dt_sum_t[hf:hf + 1, :], acs[:, hb:hb + 1] - q_t[hb:hb + 1, :]))
                ms.append((cb * jnp.exp(arg)).astype(BF16))
                xm.append(jnp.where(head_of_lane == r, x_g, jnp.zeros_like(x_g)))
            y_diag = (jnp.dot(jnp.concatenate(ms[0:2], axis=1), jnp.concatenate(xm[0:2], axis=0),
                              preferred_element_type=F32)
                      + jnp.dot(jnp.concatenate(ms[2:4], axis=1), jnp.concatenate(xm[2:4], axis=0),
                                preferred_element_type=F32))
            s_g = s_ref[g]
            y_off = jnp.dot(c_g, s_g.astype(BF16), preferred_element_type=F32) * decay_in[:, gs]
            y_ref[rows, gs] = y_diag + y_off
            upd = lax.dot_general(b_g, xw[:, gs], _TN, preferred_element_type=F32)
            s_ref[g] = s_g * decay_in[CHUNK - 1:CHUNK, gs] + upd


def _ssd_fwd(xs, bc, dt, dtb, alog, e, layer, seq):
    T = xs.shape[0]
    rows = SSD_STEP_CHUNKS * CHUNK
    assert seq % rows == 0
    row = lambda c: (c, 0)
    kern = functools.partial(_ssd_fwd_kernel, steps_per_seq=seq // rows)
    return pl.pallas_call(
        kern,
        grid=(T // rows,),
        in_specs=[pl.BlockSpec((rows, D_INNER), row), pl.BlockSpec((rows, CONV_DIM - D_INNER), row),
                  pl.BlockSpec((rows, LANES), row), _of_layer(dtb, layer, (1, LANES), 0),
                  _of_layer(alog, layer, (1, LANES), 0), _resident(e.shape)],
        out_specs=pl.BlockSpec((rows, D_INNER), row),
        out_shape=jax.ShapeDtypeStruct((T, D_INNER), F32),
        scratch_shapes=[pltpu.VMEM((SSD_GROUPS, SSD_STATE, GROUP_WIDTH), F32)],
        compiler_params=_params(),
        name="ssd_fwd",
    )(xs, bc, dt, dtb, alog, e)


def _ssd_bwd_kernel(xs_ref, bc_ref, dt_ref, dtb_ref, alog_ref, e_ref, yp_ref, z_ref, gs_ref, ga_ref, x_ref,
                    dskip_ref, nw_ref, wso_ref, bg_ref, wo_ref, pmn_ref, out_ref, s_ref, yn_ref, *,
                    steps_per_seq):
    c = pl.program_id(0)

    @pl.when(c % steps_per_seq == 0)
    def _():
        s_ref[...] = jnp.zeros_like(s_ref)

    lane = lax.broadcasted_iota(jnp.int32, (CHUNK, LANES), 1)
    a = -jnp.exp(alog_ref[...])
    nb = SSD_GROUPS * SSD_STATE
    for ci in reversed(range(SSD_STEP_CHUNKS)):
        rows = slice(ci * CHUNK, (ci + 1) * CHUNK)
        dt = _softplus(dt_ref[rows, :] + dtb_ref[...])
        acs = _tri_cumsum(dt * a, True)
        decay_in, w_state = _expand_heads(jnp.exp(acs), jnp.exp(acs[0:1, :] - acs) * dt, lane, e_ref)
        xs = xs_ref[rows, :]
        xw = (xs * w_state).astype(BF16)
        for g in range(SSD_GROUPS):
            gs = slice(g * GROUP_WIDTH, (g + 1) * GROUP_WIDTH)
            b_g = bc_ref[rows, g * SSD_STATE:(g + 1) * SSD_STATE]
            c_g = bc_ref[rows, nb + g * SSD_STATE:nb + (g + 1) * SSD_STATE]
            s_g = s_ref[g]
            y_off = jnp.dot(c_g, s_g.astype(BF16), preferred_element_type=F32) * decay_in[:, gs]
            y = yp_ref[rows, gs] + y_off + xs[:, gs] * dskip_ref[:, gs]
            z = z_ref[rows, gs]
            u = y * (z * _sigmoid(z))
            yn_ref[rows, gs] = _rms(u, nw_ref[:, gs]).astype(BF16)
            upd = lax.dot_general(b_g, xw[:, gs], _TN, preferred_element_type=F32)
            s_ref[g] = s_g * decay_in[0:1, gs] + upd
    y_ssd = jnp.dot(yn_ref[...], wso_ref[...], preferred_element_type=F32)
    gate = _sigmoid(gs_ref[...] + bg_ref[...])
    mix = (gate * y_ssd + ga_ref[...]).astype(BF16)
    mixed = jnp.dot(mix, wo_ref[...], preferred_element_type=F32)
    out_ref[...] = x_ref[...] + _rms(mixed, pmn_ref[...])


def _ssd_bwd(xs, bc, dt, dtb, alog, e, yp, zg, ga, x, dskip, nw, wso, bg, wo, pmn, layer, seq):
    T = xs.shape[0]
    rows = SSD_STEP_CHUNKS * CHUNK
    last = T // rows - 1
    rev = lambda c: (last - c, 0)
    kern = functools.partial(_ssd_bwd_kernel, steps_per_seq=seq // rows)
    return pl.pallas_call(
        kern,
        grid=(T // rows,),
        in_specs=[pl.BlockSpec((rows, D_INNER), rev), pl.BlockSpec((rows, CONV_DIM - D_INNER), rev),
                  pl.BlockSpec((rows, LANES), lambda c: (last - c, 1)),
                  _of_layer(dtb, layer, (1, LANES), 1), _of_layer(alog, layer, (1, LANES), 1),
                  _resident(e.shape),
                  pl.BlockSpec((rows, D_INNER), rev),
                  pl.BlockSpec((rows, D_INNER), rev),
                  pl.BlockSpec((rows, D_MODEL), lambda c: (last - c, 2)),
                  pl.BlockSpec((rows, D_MODEL), rev), pl.BlockSpec((rows, D_MODEL), rev),
                  _of_layer(dskip, layer), _of_layer(nw, layer), _of_layer(wso, layer),
                  _of_layer(bg, layer, (1, D_MODEL), 0),
                  _of_layer(wo, layer), _of_layer(pmn, layer)],
        out_specs=pl.BlockSpec((rows, D_MODEL), rev),
        out_shape=jax.ShapeDtypeStruct((T, D_MODEL), F32),
        scratch_shapes=[pltpu.VMEM((SSD_GROUPS, SSD_STATE, GROUP_WIDTH), F32), pltpu.VMEM((rows, D_INNER), BF16)],
        compiler_params=_params(),
        name="ssd_bwd_mix",
    )(xs, bc, dt, dtb, alog, e, yp, zg, zg, ga, x, dskip, nw, wso, bg, wo, pmn)


def _attn_kernel(q_ref, kp_ref, kc_ref, kn_ref, vp_ref, vc_ref, vn_ref, bias_ref, sink_ref, gate_ref, bg_ref,
                 wao_ref, out_ref, *, nb):
    n = pl.program_id(0)
    k_all = jnp.concatenate([kp_ref[...], kc_ref[...], kn_ref[...]], axis=0)
    v_all = jnp.concatenate([vp_ref[...], vc_ref[...], vn_ref[...]], axis=0)
    low_half = lax.broadcasted_iota(jnp.int32, (ATTN_BLOCK, LANES), 1) < HEAD_DIM
    n_tiles = KV_HEADS // 2
    rep = Q_HEADS // KV_HEADS
    half_rows = rep * ATTN_BLOCK

    def head_group(j, t, e):
        blk = n * ATTN_STEP_BLOCKS + j
        variant = ((blk % nb) == 0).astype(jnp.int32) + 2 * ((blk % nb) == nb - 1).astype(jnp.int32)
        kt = k_all[j * ATTN_BLOCK:(j + 3) * ATTN_BLOCK, t * LANES:(t + 1) * LANES]
        vt = v_all[j * ATTN_BLOCK:(j + 3) * ATTN_BLOCK, t * LANES:(t + 1) * LANES]
        keep = low_half if e == 0 else jnp.logical_not(low_half)
        qs = []
        for r in range(rep):
            qt = q_ref[j * ATTN_BLOCK:(j + 1) * ATTN_BLOCK, (rep * t + r) * LANES:(rep * t + r + 1) * LANES]
            qs.append(jnp.where(keep, qt, jnp.zeros_like(qt)))
        qz = jnp.concatenate(qs, axis=0)
        hrows = slice(e * half_rows, (e + 1) * half_rows)
        s = lax.dot_general(qz, kt, _NT, preferred_element_type=F32) + bias_ref[variant, t, hrows, :]
        sk = sink_ref[t, hrows, :]
        m = jnp.maximum(jnp.max(s, axis=-1, keepdims=True), sk)
        p = jnp.exp(s - m).astype(BF16)
        o = jnp.dot(p, jnp.concatenate([vt, jnp.ones_like(vt)], axis=1), preferred_element_type=F32)
        return o[:, :LANES] * (1.0 / (o[:, LANES:] + jnp.exp(sk - m)))

    block_outs = []
    for j in range(ATTN_STEP_BLOCKS):
        tiles = []
        for t in range(n_tiles):
            lo, hi = head_group(j, t, 0), head_group(j, t, 1)
            for r in range(rep):
                rr = slice(r * ATTN_BLOCK, (r + 1) * ATTN_BLOCK)
                tiles.append(jnp.where(low_half, lo[rr], hi[rr]).astype(BF16))
        block_outs.append(jnp.concatenate(tiles, axis=1))
    o = jnp.concatenate(block_outs, axis=0)
    y = jnp.dot(o, wao_ref[...], preferred_element_type=F32)
    out_ref[...] = _sigmoid(gate_ref[...] + bg_ref[...]) * y


def _attn(qkv, bias, sink, zg, bg, wao, layer, seq):
    T = qkv.shape[0]
    nb = seq // ATTN_BLOCK
    nblk = T // ATTN_BLOCK
    rows = ATTN_STEP_BLOCKS * ATTN_BLOCK
    assert seq % rows == 0
    kcol = ATTN_WIDTH // KV_WIDTH
    vcol = kcol + 1
    prev = lambda n: jnp.maximum(n * ATTN_STEP_BLOCKS - 1, 0)
    nxt = lambda n: jnp.minimum((n + 1) * ATTN_STEP_BLOCKS, nblk - 1)
    kern = functools.partial(_attn_kernel, nb=nb)
    return pl.pallas_call(
        kern,
        grid=(T // rows,),
        in_specs=[pl.BlockSpec((rows, ATTN_WIDTH), lambda n: (n, 0)),
                  pl.BlockSpec((ATTN_BLOCK, KV_WIDTH), lambda n: (prev(n), kcol)),
                  pl.BlockSpec((rows, KV_WIDTH), lambda n: (n, kcol)),
                  pl.BlockSpec((ATTN_BLOCK, KV_WIDTH), lambda n: (nxt(n), kcol)),
                  pl.BlockSpec((ATTN_BLOCK, KV_WIDTH), lambda n: (prev(n), vcol)),
                  pl.BlockSpec((rows, KV_WIDTH), lambda n: (n, vcol)),
                  pl.BlockSpec((ATTN_BLOCK, KV_WIDTH), lambda n: (nxt(n), vcol)),
                  _resident(bias.shape), _of_layer(sink, layer),
                  pl.BlockSpec((rows, D_MODEL), lambda n: (n, 3)),
                  _of_layer(bg, layer, (1, D_MODEL), 1),
                  _of_layer(wao, layer)],
        out_specs=pl.BlockSpec((rows, D_MODEL), lambda n: (n, 0)),
        out_shape=jax.ShapeDtypeStruct((T, D_MODEL), F32),
        compiler_params=_params(),
        name="attn",
    )(qkv, qkv, qkv, qkv, qkv, qkv, qkv, bias, sink, zg, bg, wao)


def _mlp_kernel(x_ref, g1_ref, w1_ref, w2_ref, g2_ref, out_ref):
    x = x_ref[...]
    h = _rms(x, g1_ref[...]).astype(BF16)
    step = 1024
    acc = None
    for j in range(D_FF // step):
        a = jnp.dot(h, w1_ref[:, j * step:(j + 1) * step], preferred_element_type=F32)
        a = jnp.square(jnp.maximum(a, 0.0)).astype(BF16)
        t = jnp.dot(a, w2_ref[j * step:(j + 1) * step, :], preferred_element_type=F32)
        acc = t if acc is None else acc + t
    out_ref[...] = x + _rms(acc, g2_ref[...])


def _mlp(x, g1, w1, w2, g2, layer, tm=512):
    T = x.shape[0]
    row = lambda i: (i, 0)
    return pl.pallas_call(
        _mlp_kernel,
        grid=(T // tm,),
        in_specs=[pl.BlockSpec((tm, D_MODEL), row), _of_layer(g1, layer), _of_layer(w1, layer),
                  _of_layer(w2, layer), _of_layer(g2, layer)],
        out_specs=pl.BlockSpec((tm, D_MODEL), row),
        out_shape=jax.ShapeDtypeStruct((T, D_MODEL), F32),
        compiler_params=_params(),
        name="mlp",
    )(x, g1, w1, w2, g2)


def _permute_heads(a, axis):
    shape = a.shape
    a = a.reshape(shape[:axis] + (KV_HEADS // 2, 2, 4) + shape[axis + 1:])
    a = jnp.swapaxes(a, axis + 1, axis + 2)
    return a.reshape(shape)


def _dt_layout(fwd, bwd, axis=-1):
    return jnp.concatenate([fwd, fwd, fwd, bwd, bwd, bwd, bwd, jnp.zeros_like(bwd)], axis=axis)


def _transpose_cast_kernel(w_ref, o_ref):
    o_ref[...] = w_ref[...].T.astype(BF16)


def _prep_w_in(w_in):
    depth = w_in.shape[0]
    n_in = w_in.shape[2]
    wt = jnp.swapaxes(w_in, 1, 2)
    zx = D_INNER + CONV_DIM
    q0 = zx + 2 * SSD_HEADS
    k0 = q0 + ATTN_WIDTH
    rows = WT_BLOCK
    n_blocks = (zx + rows + ATTN_WIDTH + n_in - k0) // rows
    assert zx % rows == 0 and ATTN_WIDTH % rows == 0 and (n_in - k0) % rows == 0 and n_in % HALO_ROWS == 0

    def src(l, b):
        row = l * n_in + b * rows - (b > zx // rows).astype(jnp.int32) * (zx + rows - q0)
        return (pl.multiple_of(row, HALO_ROWS), 0)

    wbig = pl.pallas_call(
        _transpose_cast_kernel,
        grid=(depth, n_blocks),
        in_specs=[pl.BlockSpec((pl.Element(rows), pl.Element(D_MODEL)), src)],
        out_specs=pl.BlockSpec((None, D_MODEL, rows), lambda l, b: (l, 0, b)),
        out_shape=jax.ShapeDtypeStruct((depth, D_MODEL, n_blocks * rows), BF16),
        compiler_params=_params(2),
        name="weight_transpose",
    )(wt.reshape(depth * n_in, D_MODEL))
    wq = wbig[:, :, Q_OFF:KVG_OFF].reshape(depth, D_MODEL, Q_HEADS, HEAD_DIM) * (HEAD_DIM ** -0.5)
    wq = _permute_heads(wq, 2).reshape(depth, D_MODEL, ATTN_WIDTH).astype(BF16)
    wdt = _dt_layout(wbig[:, :, zx:zx + SSD_HEADS], wbig[:, :, zx + SSD_HEADS:q0])
    return wbig, wq, wdt


def _expand_matrix():
    e = np.zeros((LANES, D_INNER), np.float32)
    for p in range(3):
        for h in range(SSD_HEADS):
            e[p * SSD_HEADS + h, h * SSD_HEAD_DIM:(h + 1) * SSD_HEAD_DIM] = 1.0
    return jnp.asarray(e, BF16)


def _t5_bucket(rel):
    nb = N_BUCKETS // 2
    max_exact = nb // 2
    ret = jnp.where(rel > 0, nb, 0)
    n = jnp.abs(rel)
    nf = jnp.maximum(n, 1).astype(F32)
    large = max_exact + (jnp.log(nf / max_exact) / math.log(MAX_DISTANCE / max_exact)
                         * (nb - max_exact)).astype(jnp.int32)
    large = jnp.minimum(large, nb - 1)
    return ret + jnp.where(n < max_exact, n, large)


def _window_bias(rel_table):
    period = 4 * ATTN_BLOCK
    d = jnp.arange(period)
    d = jnp.where(d >= 3 * ATTN_BLOCK, d - period, d)
    rel = d - ATTN_BLOCK
    vec = rel_table[_t5_bucket(rel)].astype(F32)
    vec = jnp.where((jnp.abs(rel) <= WINDOW)[:, None], vec, -jnp.inf).T
    skew = jnp.tile(vec, (1, ATTN_BLOCK))[:, :ATTN_BLOCK * (period - 1)]
    bias = skew.reshape(Q_HEADS, ATTN_BLOCK, period - 1)[:, :, :3 * ATTN_BLOCK]
    bias = bias.reshape(KV_HEADS // 2, 8 * ATTN_BLOCK, 3 * ATTN_BLOCK)
    col = jnp.arange(3 * ATTN_BLOCK)
    no_prev = jnp.where(col < ATTN_BLOCK, -jnp.inf, 0.0).astype(F32)
    no_next = jnp.where(col >= 2 * ATTN_BLOCK, -jnp.inf, 0.0).astype(F32)
    return jnp.stack([bias, bias + no_prev, bias + no_next, bias + no_prev + no_next])


def _stack_sink(sink):
    depth = sink.shape[0]
    return jnp.repeat(sink.astype(F32), ATTN_BLOCK, axis=1).reshape(depth, KV_HEADS // 2, 8 * ATTN_BLOCK, 1)


def kernel(x, pre_mix_norm, w_in, b_gate, conv_w, conv_b, dt_bias, a_log, d_skip, ssd_norm, w_ssd_out, attn_sink,
           rel_bias_table, w_attn_out, w_o, post_mix_norm, pre_mlp_norm, w_mlp_in, w_mlp_out, post_mlp_norm):
    bsz, seq, d = x.shape
    assert d == D_MODEL and seq % (SSD_STEP_CHUNKS * CHUNK) == 0 and seq % ATTN_BLOCK == 0
    depth = w_in.shape[0]
    xf = x.reshape(bsz * seq, d)
    row3 = lambda v: v.reshape(depth, 1, -1).astype(F32)
    e = _expand_matrix()
    bias = _window_bias(rel_bias_table)
    sink = _stack_sink(attn_sink)
    wbig, wq, wdt = _prep_w_in(w_in)
    conv_w3 = conv_w.reshape(depth, CONV_TAPS, CONV_DIM)
    dtb = row3(_dt_layout(dt_bias[:, 0], dt_bias[:, 1]))
    alog = row3(_dt_layout(a_log[:, 0], a_log[:, 1]))
    dskip = row3(jnp.repeat(d_skip, SSD_HEAD_DIM, axis=1))
    wao = _permute_heads(w_attn_out.reshape(depth, Q_HEADS, HEAD_DIM, D_MODEL), 1)
    wao = wao.reshape(depth, ATTN_WIDTH, D_MODEL).astype(BF16)
    wso, wo = w_ssd_out.astype(BF16), w_o.astype(BF16)
    w1, w2 = w_mlp_in.astype(BF16), w_mlp_out.astype(BF16)
    g_mix, g_post, g_mlp, g_out = row3(pre_mix_norm), row3(post_mix_norm), row3(pre_mlp_norm), row3(post_mlp_norm)
    conv_b3, nw, bg = row3(conv_b), row3(ssd_norm), row3(b_gate)
    for l in range(depth):
        xs, bc, zg, qkv, dtr = _inproj(xf, g_mix, wbig, l, wq, wdt, conv_w3, conv_b3, seq)
        yp = _ssd_fwd(xs, bc, dtr, dtb, alog, e, l, seq)
        ga = _attn(qkv, bias, sink, zg, bg, wao, l, seq)
        xf = _ssd_bwd(xs, bc, dtr, dtb, alog, e, yp, zg, ga, xf, dskip, nw, wso, bg, wo, g_post, l, seq)
        xf = _mlp(xf, g_mlp, w1, w2, g_out, l)
    return xf.reshape(bsz, seq, d)
```
